```python
import numpy as np
import jax
import jax.numpy as jnp
from jax import lax

D_MODEL = 2048
BATCH = 8
SEQ = 2048
DEPTH = 1
DEC_BATCH = 128
DEC_SEQ = 4
PAST_LEN = 2048
PAGE_SIZE = 128

POOL_WINDOWS = (2, 4, 8, 16)
N_POOL_GROUPS = len(POOL_WINDOWS)
POOL_WIDTH = D_MODEL // 2
POOL_GROUP = POOL_WIDTH // N_POOL_GROUPS
POOL_STATE = max(POOL_WINDOWS) - 1
N_HEADS = 16
HEAD_DIM = 128
N_KV = 4
HPG = N_HEADS // N_KV
ROT_DIM = HEAD_DIM // 4
ROPE_THETA = 500000.0
ATTN_SCALE = HEAD_DIM ** -0.5
CMP_LEN = 32
CMP_STRIDE = 16
CMP_RATIO = CMP_LEN // CMP_STRIDE
SLC_BLK = 64
N_SEL = 16
WINDOW = 512
WIN_QBLK = 128
SLC_QCHUNK = 64
PEER_HEADS = 8
N_KEYS = 128
N_EXPERTS = N_KEYS * N_KEYS
PEER_TOPK = 16
PEER_QDIM = 256
PEER_HALF = PEER_QDIM // 2
PEER_BLOCK = 64
DN_ALPHA = (2.0 * DEPTH) ** 0.25
DN_BETA = (8.0 * DEPTH) ** -0.25
LN_EPS = 1e-5

Q_WIDTH = N_HEADS * HEAD_DIM
KV_WIDTH = N_KV * HEAD_DIM
IN_SIZES = (POOL_WIDTH, Q_WIDTH, KV_WIDTH, KV_WIDTH, KV_WIDTH, KV_WIDTH, KV_WIDTH, KV_WIDTH, 3 * N_HEADS, 2 * D_MODEL)

kernel_name = 'hybrid_pool_nsa_peer_decode_step'


def layer_norm(x, g, b):
    xf = x.astype(jnp.float32)
    mu = jnp.mean(xf, axis=-1, keepdims=True)
    var = jnp.mean(jnp.square(xf - mu), axis=-1, keepdims=True)
    return ((xf - mu) * lax.rsqrt(var + LN_EPS) * g + b).astype(x.dtype)


def masked_softmax(s, mask):
    s = jnp.where(mask, s.astype(jnp.float32), -jnp.inf)
    m = jnp.max(s, axis=-1, keepdims=True)
    e = jnp.exp(s - jnp.where(jnp.isfinite(m), m, 0.0))
    den = jnp.sum(e, axis=-1, keepdims=True)
    return e / jnp.where(den > 0.0, den, 1.0)


def rope(x, pos):
    half = ROT_DIM // 2
    inv = ROPE_THETA ** (-jnp.arange(half, dtype=jnp.float32) / half)
    ang = pos.astype(jnp.float32)[:, None] * inv
    cos = jnp.cos(ang)[:, None, :]
    sin = jnp.sin(ang)[:, None, :]
    xr = x[..., :ROT_DIM].astype(jnp.float32)
    x1, x2 = xr[..., :half], xr[..., half:]
    rot = jnp.concatenate([x1 * cos - x2 * sin, x2 * cos + x1 * sin], axis=-1).astype(x.dtype)
    return jnp.concatenate([rot, x[..., ROT_DIM:]], axis=-1)


def pool_mixer(u, u_prev, pos0, w_grp, scale):
    B, L, P = u.shape
    ext = jnp.concatenate([u_prev, u], axis=1)
    cs = jnp.pad(jnp.cumsum(ext.astype(jnp.float32), axis=1), ((0, 0), (1, 0), (0, 0)))
    pos = pos0 + np.arange(L)
    upto = cs[:, POOL_STATE + 1:]
    means = []
    for gi, w in enumerate(POOL_WINDOWS):
        c = slice(gi * POOL_GROUP, (gi + 1) * POOL_GROUP)
        before = cs[:, POOL_STATE + 1 - w:POOL_STATE + 1 - w + L, c]
        cnt = np.minimum(w, pos + 1).astype(np.float32)[None, :, None]
        means.append((upto[..., c] - before) / cnt)
    d = (jnp.concatenate(means, axis=-1) - u.astype(jnp.float32)).astype(u.dtype)
    d = d.reshape(B, L, N_POOL_GROUPS, POOL_GROUP)
    y = jnp.einsum('blgc,gce->blge', d, w_grp).reshape(B, L, P) * scale
    return y, ext[:, -POOL_STATE:]


def compress(kv, pe, w1, w2):
    B, Lt = kv.shape[:2]
    n_chunk = Lt // CMP_STRIDE
    n_cmp = n_chunk - CMP_RATIO + 1
    ch = kv[:, :n_chunk * CMP_STRIDE].reshape(B, n_chunk, CMP_STRIDE, N_KV, HEAD_DIM)
    w1r = w1.reshape(CMP_RATIO, CMP_STRIDE, HEAD_DIM, HEAD_DIM)
    a = jnp.einsum('bcsgd,rsde->rbcge', ch, w1r)
    hid = jnp.einsum('jd,jde->e', pe, w1)
    for r in range(CMP_RATIO):
        hid = hid + a[r, :, r:r + n_cmp]
    return jnp.einsum('bnge,ef->bngf', jax.nn.gelu(hid, approximate=False), w2)


def cmp_attend(q, qpos, kcb, vcb):
    B, L = q.shape[:2]
    n_cmp = kcb.shape[1]
    qg = q.reshape(B, L, N_KV, HPG, HEAD_DIM)
    s = jnp.einsum('blghd,bngd->bghln', qg, kcb) * ATTN_SCALE
    ends = np.arange(n_cmp) * CMP_STRIDE + CMP_LEN - 1
    pr = masked_softmax(s, ends[None, :] <= qpos[:, None])
    o = jnp.einsum('bghln,bngd->blghd', pr.astype(vcb.dtype), vcb).reshape(B, L, N_HEADS, HEAD_DIM)
    return o, jnp.sum(pr, axis=2)


def select_blocks(imp_c, qpos, Lt):
    n_cmp = imp_c.shape[-1]
    n_slc = -(-Lt // SLC_BLK)
    cs = np.arange(n_cmp) * CMP_STRIDE
    bs = np.arange(n_slc) * SLC_BLK
    inter = ((cs[:, None] < bs[None, :] + SLC_BLK) & (cs[:, None] + CMP_LEN > bs[None, :])).astype(np.float32)
    imp = jnp.einsum('bgln,nj->bglj', imp_c, jnp.asarray(inter))
    cur = (qpos // SLC_BLK)[:, None]
    j = np.arange(n_slc)[None, :]
    valid = bs[None, :] <= qpos[:, None]
    forced = valid & ((j == 0) | (j == cur) | (j == cur - 1))
    score = jnp.where(forced, jnp.inf, jnp.where(valid, imp, -jnp.inf))
    _, idx = lax.top_k(score, min(N_SEL, n_slc))
    return idx


def slc_attend(q, qpos, idx, k, v):
    B, L = q.shape[:2]
    Lt = k.shape[1]
    n_slc = -(-Lt // SLC_BLK)
    padk = ((0, 0), (0, n_slc * SLC_BLK - Lt), (0, 0), (0, 0))

    def blocks(a):
        return jnp.pad(a, padk).reshape(B, n_slc, SLC_BLK, N_KV, HEAD_DIM).transpose(0, 3, 1, 2, 4)

    kb, vb = blocks(k), blocks(v)
    n_q = B * L
    n_pad = (-n_q) % SLC_QCHUNK

    def flat(a):
        a = jnp.pad(a, [(0, n_pad)] + [(0, 0)] * (a.ndim - 1))
        return a.reshape((-1, SLC_QCHUNK) + a.shape[1:])

    qf = flat(q.reshape(n_q, N_KV, HPG, HEAD_DIM))
    idf = flat(idx.transpose(0, 2, 1, 3).reshape(n_q, N_KV, -1))
    sid = flat(jnp.repeat(jnp.arange(B, dtype=jnp.int32), L))
    pf = flat(jnp.asarray(np.broadcast_to(qpos, (B, L)).reshape(-1), jnp.int32))
    g_ix = jnp.arange(N_KV)[None, :, None]

    def step(args):
        qc, ic, sc, pc = args
        C, _, n_sel = ic.shape
        kg = kb[sc[:, None, None], g_ix, ic]
        vg = vb[sc[:, None, None], g_ix, ic]
        s = jnp.einsum('cghd,cgnkd->cghnk', qc, kg).reshape(C, N_KV, HPG, n_sel * SLC_BLK) * ATTN_SCALE
        kpos = ic[..., None] * SLC_BLK + jnp.arange(SLC_BLK)
        mask = (kpos <= pc[:, None, None, None]).reshape(C, N_KV, 1, n_sel * SLC_BLK)
        pr = masked_softmax(s, mask)
        vflat = vg.reshape(C, N_KV, n_sel * SLC_BLK, HEAD_DIM)
        return jnp.einsum('cghm,cgmd->cghd', pr.astype(vg.dtype), vflat)

    o = lax.map(step, (qf, idf, sid, pf))
    return o.reshape(-1, N_HEADS, HEAD_DIM)[:n_q].reshape(B, L, N_HEADS, HEAD_DIM)


def window_attend(q, k_lead, v_lead, k_new, v_new, pos0):
    B, L = q.shape[:2]
    kf = jnp.concatenate([k_lead, k_new], axis=1)
    vf = jnp.concatenate([v_lead, v_new], axis=1)
    kpos = jnp.asarray(pos0 - WINDOW + np.arange(WINDOW + L), jnp.int32)
    qb = WIN_QBLK if L % WIN_QBLK == 0 else L
    nqb = L // qb
    qblocks = q.reshape(B, nqb, qb, N_KV, HPG, HEAD_DIM).transpose(1, 0, 2, 3, 4, 5)

    def step(args):
        b, qc = args
        start = b * qb
        kc = lax.dynamic_slice_in_dim(kf, start, WINDOW + qb, axis=1)
        vc = lax.dynamic_slice_in_dim(vf, start, WINDOW + qb, axis=1)
        kp = lax.dynamic_slice_in_dim(kpos, start, WINDOW + qb, axis=0)[None, :]
        qp = (pos0 + start + jnp.arange(qb))[:, None]
        s = jnp.einsum('bighd,bkgd->bghik', qc, kc) * ATTN_SCALE
        mask = (kp <= qp) & (kp > qp - WINDOW) & (kp >= 0)
        pr = masked_softmax(s, mask)
        return jnp.einsum('bghik,bkgd->bighd', pr.astype(vc.dtype), vc)

    o = lax.map(step, (jnp.arange(nqb, dtype=jnp.int32), qblocks))
    return o.transpose(1, 0, 2, 3, 4, 5).reshape(B, L, N_HEADS, HEAD_DIM)


def peer(h, wq, sub_keys, u_tab, v_tab):
    B, L, D = h.shape
    t = h.reshape(B * L, D)
    T = t.shape[0]
    q = jnp.dot(t, wq).reshape(T, PEER_HEADS, 2, PEER_HALF)
    s = jnp.einsum('thpc,hpkc->thpk', q, sub_keys).astype(jnp.float32)
    s1, i1 = lax.top_k(s[:, :, 0], PEER_TOPK)
    s2, i2 = lax.top_k(s[:, :, 1], PEER_TOPK)
    cand = (s1[..., :, None] + s2[..., None, :]).reshape(T, PEER_HEADS, PEER_TOPK * PEER_TOPK)
    cid = (i1[..., :, None] * N_KEYS + i2[..., None, :]).reshape(T, PEER_HEADS, PEER_TOPK * PEER_TOPK)
    top, sel = lax.top_k(cand, PEER_TOPK)
    eid = jnp.take_along_axis(cid, sel, axis=-1)
    gate = jax.nn.softmax(top, axis=-1)
    n_pad = (-T) % PEER_BLOCK

    def chunks(a):
        a = jnp.pad(a, [(0, n_pad)] + [(0, 0)] * (a.ndim - 1))
        return a.reshape((-1, PEER_BLOCK) + a.shape[1:])

    def step(args):
        tc, ec, gc = args
        act = jax.nn.gelu(jnp.einsum('cd,chkd->chk', tc, u_tab[ec]).astype(jnp.float32), approximate=False)
        return jnp.einsum('chk,chkd->cd', (gc * act).astype(tc.dtype), v_tab[ec])

    out = lax.map(step, (chunks(t), chunks(eid), chunks(gate)))
    return out.reshape(-1, D)[:T].reshape(B, L, D)


def decoder_layer(x, pos0, past, win_k, win_v, pool_prev, win_rows, p):
    B, L, _ = x.shape
    dt = x.dtype
    qpos = pos0 + np.arange(L)
    qpos_j = jnp.asarray(qpos, jnp.int32)
    proj = jnp.einsum('bld,dc->blc', x, p['w_in'])
    splits = np.cumsum(IN_SIZES)[:-1].tolist()
    u_pool, q, kc, vc, ks, vs, kw, vw, g_nsa, g_br = jnp.split(proj, splits, axis=-1)
    q = q.reshape(B, L, N_HEADS, HEAD_DIM)
    kc, vc, ks, vs, kw, vw = [a.reshape(B, L, N_KV, HEAD_DIM) for a in (kc, vc, ks, vs, kw, vw)]
    q_rot = rope(q, qpos_j)
    ks = rope(ks, qpos_j)
    kw = rope(kw, qpos_j)
    if past is None:
        kc_all, vc_all, ks_all, vs_all = kc, vc, ks, vs
    else:
        kc_all, vc_all, ks_all, vs_all = [jnp.concatenate([a, b], axis=1) for a, b in zip(past, (kc, vc, ks, vs))]
    kcb = compress(kc_all, p['cmp_pe_k'], p['cmp_w1_k'], p['cmp_w2_k'])
    vcb = compress(vc_all, p['cmp_pe_v'], p['cmp_w1_v'], p['cmp_w2_v'])
    o_cmp, imp = cmp_attend(q, qpos, kcb, vcb)
    idx = select_blocks(imp, qpos, ks_all.shape[1])
    o_slc = slc_attend(q_rot, qpos, idx, ks_all, vs_all)
    lead = ((0, 0), (WINDOW - win_k.shape[1], 0), (0, 0), (0, 0))
    o_win = window_attend(q_rot, jnp.pad(win_k, lead), jnp.pad(win_v, lead), kw, vw, pos0)
    g = jax.nn.sigmoid(g_nsa.astype(jnp.float32)).astype(dt).reshape(B, L, 3, N_HEADS, 1)
    o_attn = (g[:, :, 0] * o_cmp + g[:, :, 1] * o_slc + g[:, :, 2] * o_win).reshape(B, L, Q_WIDTH)
    o_pool, pool_new = pool_mixer(u_pool, pool_prev, pos0, p['pool_grp_w'], p['pool_scale'])
    gate = jax.nn.sigmoid(g_br.astype(jnp.float32)).astype(dt)
    merged = gate[..., :D_MODEL] * (o_pool @ p['w_branch_pool']) + gate[..., D_MODEL:] * (o_attn @ p['w_branch_attn'])
    h = layer_norm(DN_ALPHA * x + merged @ p['w_out'], p['ln1_g'], p['ln1_b'])
    z = peer(h, p['peer_wq'], p['peer_subkeys'], p['peer_u'], p['peer_v'])
    y = layer_norm(DN_ALPHA * h + z, p['ln2_g'], p['ln2_b'])
    win_k_new = jnp.concatenate([win_k, kw], axis=1)[:, -win_rows:]
    win_v_new = jnp.concatenate([win_v, vw], axis=1)[:, -win_rows:]
    return y, (kc, vc, ks, vs, win_k_new, win_v_new, pool_new)


def setup_inputs(seed: int = 0) -> dict:
    key = jax.random.key(seed)
    keys = iter(jax.random.split(key, 48))

    def nrm(shape, s=1.0):
        return jax.random.normal(next(keys), shape, jnp.float32) * s

    n_pages = PAST_LEN // PAGE_SIZE
    n_used = DEC_BATCH * n_pages
    n_phys = n_used + max(1, n_used // 4)
    perm = jax.random.permutation(next(keys), n_phys).astype(jnp.int32)
    page_table = perm[:n_used].reshape(DEC_BATCH, n_pages)
    win_rows = min(WINDOW, PAST_LEN)
    fi = D_MODEL ** -0.5
    paged = (DEPTH, n_phys, PAGE_SIZE, N_KV, HEAD_DIM)
    w_in = jnp.concatenate([
        nrm((DEPTH, D_MODEL, POOL_WIDTH), fi),
        nrm((DEPTH, D_MODEL, Q_WIDTH), fi),
        nrm((DEPTH, D_MODEL, KV_WIDTH), fi),
        nrm((DEPTH, D_MODEL, KV_WIDTH), fi * DN_BETA),
        nrm((DEPTH, D_MODEL, KV_WIDTH), fi),
        nrm((DEPTH, D_MODEL, KV_WIDTH), fi * DN_BETA),
        nrm((DEPTH, D_MODEL, KV_WIDTH), fi),
        nrm((DEPTH, D_MODEL, KV_WIDTH), fi * DN_BETA),
        nrm((DEPTH, D_MODEL, 3 * N_HEADS), fi),
        nrm((DEPTH, D_MODEL, 2 * D_MODEL), fi),
    ], axis=-1)
    return {
        'x_prompt': nrm((BATCH, SEQ, D_MODEL)),
        'x_sample': nrm((DEC_BATCH, DEC_SEQ, D_MODEL)),
        'cache_k_cmp': nrm(paged),
        'cache_v_cmp': nrm(paged),
        'cache_k_slc': nrm(paged),
        'cache_v_slc': nrm(paged),
        'cache_k_win': nrm((DEPTH, DEC_BATCH, win_rows, N_KV, HEAD_DIM)),
        'cache_v_win': nrm((DEPTH, DEC_BATCH, win_rows, N_KV, HEAD_DIM)),
        'state_pool': nrm((DEPTH, DEC_BATCH, POOL_STATE, POOL_WIDTH)),
        'page_table': page_table,
        'w_in': w_in,
        'pool_grp_w': nrm((DEPTH, N_POOL_GROUPS, POOL_GROUP, POOL_GROUP), POOL_GROUP ** -0.5),
        'pool_scale': 1.0 + nrm((DEPTH, POOL_WIDTH), 0.1),
        'cmp_pe_k': nrm((DEPTH, CMP_LEN, HEAD_DIM), 0.5),
        'cmp_w1_k': nrm((DEPTH, CMP_LEN, HEAD_DIM, HEAD_DIM), (CMP_LEN * HEAD_DIM) ** -0.5),
        'cmp_w2_k': nrm((DEPTH, HEAD_DIM, HEAD_DIM), HEAD_DIM ** -0.5),
        'cmp_pe_v': nrm((DEPTH, CMP_LEN, HEAD_DIM), 0.5),
        'cmp_w1_v': nrm((DEPTH, CMP_LEN, HEAD_DIM, HEAD_DIM), (CMP_LEN * HEAD_DIM) ** -0.5),
        'cmp_w2_v': nrm((DEPTH, HEAD_DIM, HEAD_DIM), HEAD_DIM ** -0.5),
        'w_branch_pool': nrm((DEPTH, POOL_WIDTH, D_MODEL), POOL_WIDTH ** -0.5 * DN_BETA),
        'w_branch_attn': nrm((DEPTH, Q_WIDTH, D_MODEL), Q_WIDTH ** -0.5 * DN_BETA),
        'w_out': nrm((DEPTH, D_MODEL, D_MODEL), fi * DN_BETA),
        'ln1_g': 1.0 + nrm((DEPTH, D_MODEL), 0.02),
        'ln1_b': nrm((DEPTH, D_MODEL), 0.02),
        'peer_wq': nrm((DEPTH, D_MODEL, PEER_HEADS * PEER_QDIM), fi),
        'peer_subkeys': nrm((DEPTH, PEER_HEADS, 2, N_KEYS, PEER_HALF), PEER_HALF ** -0.5),
        'peer_u': nrm((DEPTH, N_EXPERTS, D_MODEL), fi * DN_BETA),
        'peer_v': nrm((DEPTH, N_EXPERTS, D_MODEL), DN_BETA * PEER_HEADS ** -0.5),
        'ln2_g': 1.0 + nrm((DEPTH, D_MODEL), 0.02),
        'ln2_b': nrm((DEPTH, D_MODEL), 0.02),
    }


def reference(x_prompt, x_sample, cache_k_cmp, cache_v_cmp, cache_k_slc, cache_v_slc, cache_k_win, cache_v_win, state_pool, page_table, w_in, pool_grp_w, pool_scale, cmp_pe_k, cmp_w1_k, cmp_w2_k, cmp_pe_v, cmp_w1_v, cmp_w2_v, w_branch_pool, w_branch_attn, w_out, ln1_g, ln1_b, peer_wq, peer_subkeys, peer_u, peer_v, ln2_g, ln2_b):
    B, L = x_prompt.shape[:2]
    zero_win = jnp.zeros((B, 0, N_KV, HEAD_DIM), x_prompt.dtype)
    zero_pool = jnp.zeros((B, POOL_STATE, POOL_WIDTH), x_prompt.dtype)

    def gather_pages(c):
        g = c[page_table]
        return g.reshape(g.shape[0], -1, N_KV, HEAD_DIM)

    hp, hs = x_prompt, x_sample
    st_prompt, st_sample = [], []
    for l in range(DEPTH):
        p = {'w_in': w_in[l], 'pool_grp_w': pool_grp_w[l], 'pool_scale': pool_scale[l],
             'cmp_pe_k': cmp_pe_k[l], 'cmp_w1_k': cmp_w1_k[l], 'cmp_w2_k': cmp_w2_k[l],
             'cmp_pe_v': cmp_pe_v[l], 'cmp_w1_v': cmp_w1_v[l], 'cmp_w2_v': cmp_w2_v[l],
             'w_branch_pool': w_branch_pool[l], 'w_branch_attn': w_branch_attn[l], 'w_out': w_out[l],
             'ln1_g': ln1_g[l], 'ln1_b': ln1_b[l], 'peer_wq': peer_wq[l], 'peer_subkeys': peer_subkeys[l],
             'peer_u': peer_u[l], 'peer_v': peer_v[l], 'ln2_g': ln2_g[l], 'ln2_b': ln2_b[l]}
        hp, sp = decoder_layer(hp, 0, None, zero_win, zero_win, zero_pool, min(WINDOW, L), p)
        past = (gather_pages(cache_k_cmp[l]), gather_pages(cache_v_cmp[l]),
                gather_pages(cache_k_slc[l]), gather_pages(cache_v_slc[l]))
        hs, ss = decoder_layer(hs, PAST_LEN, past, cache_k_win[l], cache_v_win[l], state_pool[l],
                               cache_k_win.shape[2], p)
        st_prompt.append(sp)
        st_sample.append(ss)

    def stack(states, i):
        return jnp.stack([s[i] for s in states], axis=0)

    k_cmp_p = stack(st_prompt, 0)
    v_cmp_p = stack(st_prompt, 1)
    k_slc_p = stack(st_prompt, 2)
    v_slc_p = stack(st_prompt, 3)
    k_win_p = stack(st_prompt, 4)
    v_win_p = stack(st_prompt, 5)
    pool_p = stack(st_prompt, 6)
    k_cmp_s = stack(st_sample, 0)
    v_cmp_s = stack(st_sample, 1)
    k_slc_s = stack(st_sample, 2)
    v_slc_s = stack(st_sample, 3)
    k_win_s = stack(st_sample, 4)
    v_win_s = stack(st_sample, 5)
    pool_s = stack(st_sample, 6)
    return (hp, hs, k_cmp_p, v_cmp_p, k_slc_p, v_slc_p, k_win_p, v_win_p, pool_p, k_cmp_s, v_cmp_s, k_slc_s, v_slc_s, k_win_s, v_win_s, pool_s)
```

```python
import numpy as np
import jax
import jax.numpy as jnp
from jax import lax
from jax.experimental import pallas as pl
from jax.experimental.pallas import tpu as pltpu


D_MODEL = 2048
BATCH = 8
SEQ = 2048
DEPTH = 1
DEC_BATCH = 128
DEC_SEQ = 4
PAST_LEN = 2048
PAGE_SIZE = 128

POOL_WINDOWS = (2, 4, 8, 16)
N_POOL_GROUPS = len(POOL_WINDOWS)
POOL_WIDTH = D_MODEL // 2
POOL_GROUP = POOL_WIDTH // N_POOL_GROUPS
POOL_STATE = max(POOL_WINDOWS) - 1
N_HEADS = 16
HEAD_DIM = 128
N_KV = 4
HPG = N_HEADS // N_KV
ROT_DIM = HEAD_DIM // 4
ROPE_THETA = 500000.0
ATTN_SCALE = HEAD_DIM ** -0.5
CMP_LEN = 32
CMP_STRIDE = 16
CMP_RATIO = CMP_LEN // CMP_STRIDE
SLC_BLK = 64
N_SEL = 16
WINDOW = 512
WIN_QBLK = 128
SLC_QCHUNK = 64
PEER_HEADS = 8
N_KEYS = 128
N_EXPERTS = N_KEYS * N_KEYS
PEER_TOPK = 16
PEER_QDIM = 256
PEER_HALF = PEER_QDIM // 2
PEER_BLOCK = 64
DN_ALPHA = (2.0 * DEPTH) ** 0.25
DN_BETA = (8.0 * DEPTH) ** -0.25
LN_EPS = 1e-5

Q_WIDTH = N_HEADS * HEAD_DIM
KV_WIDTH = N_KV * HEAD_DIM
IN_SIZES = (POOL_WIDTH, Q_WIDTH, KV_WIDTH, KV_WIDTH, KV_WIDTH, KV_WIDTH, KV_WIDTH, KV_WIDTH, 3 * N_HEADS, 2 * D_MODEL)

V7X_LANES = 128


def _matmul_kernel(a_ref, b_ref, o_ref):
    o_ref[...] = jnp.dot(a_ref[...], b_ref[...], preferred_element_type=jnp.float32)


def matmul_bf16(a, b, tm, tn):
    M, K = a.shape
    _, N = b.shape
    tm, tn = min(tm, M), min(tn, N)
    assert M % tm == 0 and N % tn == 0
    vmem = 2 * (tm * K * 2 + K * tn * 2 + tm * tn * 4) + (4 << 20)
    return pl.pallas_call(
        _matmul_kernel,
        grid=(M // tm, N // tn),
        in_specs=[pl.BlockSpec((tm, K), lambda i, j: (i, 0)),
                  pl.BlockSpec((K, tn), lambda i, j: (0, j))],
        out_specs=pl.BlockSpec((tm, tn), lambda i, j: (i, j)),
        out_shape=jax.ShapeDtypeStruct((M, N), jnp.float32),
        compiler_params=pltpu.CompilerParams(
            dimension_semantics=("parallel", "arbitrary"), vmem_limit_bytes=vmem),
        name="matmul_bf16",
    )(a, b)


def input_projection(x, w_in):
    B, L, D = x.shape
    xb = x.reshape(B * L, D).astype(jnp.bfloat16)
    g0 = sum(IN_SIZES[:8])
    g1 = g0 + IN_SIZES[8]
    w_main = jnp.concatenate([w_in[:, :g0], w_in[:, g1:]], axis=1).astype(jnp.bfloat16)
    w_gate = jnp.pad(w_in[:, g0:g1], ((0, 0), (0, V7X_LANES - IN_SIZES[8]))).astype(jnp.bfloat16)
    main = matmul_bf16(xb, w_main, 1024, 512)
    gate = matmul_bf16(xb, w_gate, 1024, V7X_LANES)
    proj = jnp.concatenate([main[:, :g0], gate[:, :IN_SIZES[8]], main[:, g0:]], axis=1)
    return proj.reshape(B, L, -1)


def layer_norm(x, g, b):
    xf = x.astype(jnp.float32)
    mu = jnp.mean(xf, axis=-1, keepdims=True)
    var = jnp.mean(jnp.square(xf - mu), axis=-1, keepdims=True)
    return ((xf - mu) * lax.rsqrt(var + LN_EPS) * g + b).astype(x.dtype)


def masked_softmax(s, mask):
    s = jnp.where(mask, s.astype(jnp.float32), -jnp.inf)
    m = jnp.max(s, axis=-1, keepdims=True)
    e = jnp.exp(s - jnp.where(jnp.isfinite(m), m, 0.0))
    den = jnp.sum(e, axis=-1, keepdims=True)
    return e / jnp.where(den > 0.0, den, 1.0)


def rope(x, pos):
    half = ROT_DIM // 2
    inv = ROPE_THETA ** (-jnp.arange(half, dtype=jnp.float32) / half)
    ang = pos.astype(jnp.float32)[:, None] * inv
    cos = jnp.cos(ang)[:, None, :]
    sin = jnp.sin(ang)[:, None, :]
    xr = x[..., :ROT_DIM].astype(jnp.float32)
    x1, x2 = xr[..., :half], xr[..., half:]
    rot = jnp.concatenate([x1 * cos - x2 * sin, x2 * cos + x1 * sin], axis=-1).astype(x.dtype)
    return jnp.concatenate([rot, x[..., ROT_DIM:]], axis=-1)


def pool_mixer(u, u_prev, pos0, w_grp, scale):
    B, L, P = u.shape
    ext = jnp.concatenate([u_prev, u], axis=1)
    cs = jnp.pad(jnp.cumsum(ext.astype(jnp.float32), axis=1), ((0, 0), (1, 0), (0, 0)))
    pos = pos0 + np.arange(L)
    upto = cs[:, POOL_STATE + 1:]
    means = []
    for gi, w in enumerate(POOL_WINDOWS):
        c = slice(gi * POOL_GROUP, (gi + 1) * POOL_GROUP)
        before = cs[:, POOL_STATE + 1 - w:POOL_STATE + 1 - w + L, c]
        cnt = np.minimum(w, pos + 1).astype(np.float32)[None, :, None]
        means.append((upto[..., c] - before) / cnt)
    d = (jnp.concatenate(means, axis=-1) - u.astype(jnp.float32)).astype(u.dtype)
    d = d.reshape(B, L, N_POOL_GROUPS, POOL_GROUP)
    y = jnp.einsum('blgc,gce->blge', d, w_grp).reshape(B, L, P) * scale
    return y, ext[:, -POOL_STATE:]


def compress(kv, pe, w1, w2):
    B, Lt = kv.shape[:2]
    n_chunk = Lt // CMP_STRIDE
    n_cmp = n_chunk - CMP_RATIO + 1
    ch = kv[:, :n_chunk * CMP_STRIDE].reshape(B, n_chunk, CMP_STRIDE, N_KV, HEAD_DIM)
    w1r = w1.reshape(CMP_RATIO, CMP_STRIDE, HEAD_DIM, HEAD_DIM)
    a = jnp.einsum('bcsgd,rsde->rbcge', ch, w1r)
    hid = jnp.einsum('jd,jde->e', pe, w1)
    for r in range(CMP_RATIO):
        hid = hid + a[r, :, r:r + n_cmp]
    return jnp.einsum('bnge,ef->bngf', jax.nn.gelu(hid, approximate=False), w2)


def cmp_attend(q, qpos, kcb, vcb):
    B, L = q.shape[:2]
    n_cmp = kcb.shape[1]
    qg = q.reshape(B, L, N_KV, HPG, HEAD_DIM)
    s = jnp.einsum('blghd,bngd->bghln', qg, kcb) * ATTN_SCALE
    ends = np.arange(n_cmp) * CMP_STRIDE + CMP_LEN - 1
    pr = masked_softmax(s, ends[None, :] <= qpos[:, None])
    o = jnp.einsum('bghln,bngd->blghd', pr.astype(vcb.dtype), vcb).reshape(B, L, N_HEADS, HEAD_DIM)
    return o, jnp.sum(pr, axis=2)


def select_blocks(imp_c, qpos, Lt):
    n_cmp = imp_c.shape[-1]
    n_slc = -(-Lt // SLC_BLK)
    cs = np.arange(n_cmp) * CMP_STRIDE
    bs = np.arange(n_slc) * SLC_BLK
    inter = ((cs[:, None] < bs[None, :] + SLC_BLK) & (cs[:, None] + CMP_LEN > bs[None, :])).astype(np.float32)
    imp = jnp.einsum('bgln,nj->bglj', imp_c, jnp.asarray(inter))
    cur = (qpos // SLC_BLK)[:, None]
    j = np.arange(n_slc)[None, :]
    valid = bs[None, :] <= qpos[:, None]
    forced = valid & ((j == 0) | (j == cur) | (j == cur - 1))
    score = jnp.where(forced, jnp.inf, jnp.where(valid, imp, -jnp.inf))
    _, idx = lax.top_k(score, min(N_SEL, n_slc))
    return idx


def slc_attend(q, qpos, idx, k, v):
    B, L = q.shape[:2]
    Lt = k.shape[1]
    n_slc = -(-Lt // SLC_BLK)
    padk = ((0, 0), (0, n_slc * SLC_BLK - Lt), (0, 0), (0, 0))

    def blocks(a):
        return jnp.pad(a, padk).reshape(B, n_slc, SLC_BLK, N_KV, HEAD_DIM).transpose(0, 3, 1, 2, 4)

    kb, vb = blocks(k), blocks(v)
    n_q = B * L
    n_pad = (-n_q) % SLC_QCHUNK

    def flat(a):
        a = jnp.pad(a, [(0, n_pad)] + [(0, 0)] * (a.ndim - 1))
        return a.reshape((-1, SLC_QCHUNK) + a.shape[1:])

    qf = flat(q.reshape(n_q, N_KV, HPG, HEAD_DIM))
    idf = flat(idx.transpose(0, 2, 1, 3).reshape(n_q, N_KV, -1))
    sid = flat(jnp.repeat(jnp.arange(B, dtype=jnp.int32), L))
    pf = flat(jnp.asarray(np.broadcast_to(qpos, (B, L)).reshape(-1), jnp.int32))
    g_ix = jnp.arange(N_KV)[None, :, None]

    def step(args):
        qc, ic, sc, pc = args
        C, _, n_sel = ic.shape
        kg = kb[sc[:, None, None], g_ix, ic]
        vg = vb[sc[:, None, None], g_ix, ic]
        s = jnp.einsum('cghd,cgnkd->cghnk', qc, kg).reshape(C, N_KV, HPG, n_sel * SLC_BLK) * ATTN_SCALE
        kpos = ic[..., None] * SLC_BLK + jnp.arange(SLC_BLK)
        mask = (kpos <= pc[:, None, None, None]).reshape(C, N_KV, 1, n_sel * SLC_BLK)
        pr = masked_softmax(s, mask)
        vflat = vg.reshape(C, N_KV, n_sel * SLC_BLK, HEAD_DIM)
        return jnp.einsum('cghm,cgmd->cghd', pr.astype(vg.dtype), vflat)

    o = lax.map(step, (qf, idf, sid, pf))
    return o.reshape(-1, N_HEADS, HEAD_DIM)[:n_q].reshape(B, L, N_HEADS, HEAD_DIM)


def window_attend(q, k_lead, v_lead, k_new, v_new, pos0):
    B, L = q.shape[:2]
    kf = jnp.concatenate([k_lead, k_new], axis=1)
    vf = jnp.concatenate([v_lead, v_new], axis=1)
    kpos = jnp.asarray(pos0 - WINDOW + np.arange(WINDOW + L), jnp.int32)
    qb = WIN_QBLK if L % WIN_QBLK == 0 else L
    nqb = L // qb
    qblocks = q.reshape(B, nqb, qb, N_KV, HPG, HEAD_DIM).transpose(1, 0, 2, 3, 4, 5)

    def step(args):
        b, qc = args
        start = b * qb
        kc = lax.dynamic_slice_in_dim(kf, start, WINDOW + qb, axis=1)
        vc = lax.dynamic_slice_in_dim(vf, start, WINDOW + qb, axis=1)
        kp = lax.dynamic_slice_in_dim(kpos, start, WINDOW + qb, axis=0)[None, :]
        qp = (pos0 + start + jnp.arange(qb))[:, None]
        s = jnp.einsum('bighd,bkgd->bghik', qc, kc) * ATTN_SCALE
        mask = (kp <= qp) & (kp > qp - WINDOW) & (kp >= 0)
        pr = masked_softmax(s, mask)
        return jnp.einsum('bghik,bkgd->bighd', pr.astype(vc.dtype), vc)

    o = lax.map(step, (jnp.arange(nqb, dtype=jnp.int32), qblocks))
    return o.transpose(1, 0, 2, 3, 4, 5).reshape(B, L, N_HEADS, HEAD_DIM)


def peer(h, wq, sub_keys, u_tab, v_tab):
    B, L, D = h.shape
    t = h.reshape(B * L, D)
    T = t.shape[0]
    q = jnp.dot(t, wq).reshape(T, PEER_HEADS, 2, PEER_HALF)
    s = jnp.einsum('thpc,hpkc->thpk', q, sub_keys).astype(jnp.float32)
    s1, i1 = lax.top_k(s[:, :, 0], PEER_TOPK)
    s2, i2 = lax.top_k(s[:, :, 1], PEER_TOPK)
    cand = (s1[..., :, None] + s2[..., None, :]).reshape(T, PEER_HEADS, PEER_TOPK * PEER_TOPK)
    cid = (i1[..., :, None] * N_KEYS + i2[..., None, :]).reshape(T, PEER_HEADS, PEER_TOPK * PEER_TOPK)
    top, sel = lax.top_k(cand, PEER_TOPK)
    eid = jnp.take_along_axis(cid, sel, axis=-1)
    gate = jax.nn.softmax(top, axis=-1)
    n_pad = (-T) % PEER_BLOCK

    def chunks(a):
        a = jnp.pad(a, [(0, n_pad)] + [(0, 0)] * (a.ndim - 1))
        return a.reshape((-1, PEER_BLOCK) + a.shape[1:])

    def step(args):
        tc, ec, gc = args
        act = jax.nn.gelu(jnp.einsum('cd,chkd->chk', tc, u_tab[ec]).astype(jnp.float32), approximate=False)
        return jnp.einsum('chk,chkd->cd', (gc * act).astype(tc.dtype), v_tab[ec])

    out = lax.map(step, (chunks(t), chunks(eid), chunks(gate)))
    return out.reshape(-1, D)[:T].reshape(B, L, D)


def decoder_layer(x, pos0, past, win_k, win_v, pool_prev, win_rows, p):
    B, L, _ = x.shape
    dt = x.dtype
    qpos = pos0 + np.arange(L)
    qpos_j = jnp.asarray(qpos, jnp.int32)
    proj = input_projection(x, p['w_in'])
    splits = np.cumsum(IN_SIZES)[:-1].tolist()
    u_pool, q, kc, vc, ks, vs, kw, vw, g_nsa, g_br = jnp.split(proj, splits, axis=-1)
    q = q.reshape(B, L, N_HEADS, HEAD_DIM)
    kc, vc, ks, vs, kw, vw = [a.reshape(B, L, N_KV, HEAD_DIM) for a in (kc, vc, ks, vs, kw, vw)]
    q_rot = rope(q, qpos_j)
    ks = rope(ks, qpos_j)
    kw = rope(kw, qpos_j)
    if past is None:
        kc_all, vc_all, ks_all, vs_all = kc, vc, ks, vs
    else:
        kc_all, vc_all, ks_all, vs_all = [jnp.concatenate([a, b], axis=1) for a, b in zip(past, (kc, vc, ks, vs))]
    kcb = compress(kc_all, p['cmp_pe_k'], p['cmp_w1_k'], p['cmp_w2_k'])
    vcb = compress(vc_all, p['cmp_pe_v'], p['cmp_w1_v'], p['cmp_w2_v'])
    o_cmp, imp = cmp_attend(q, qpos, kcb, vcb)
    idx = select_blocks(imp, qpos, ks_all.shape[1])
    o_slc = slc_attend(q_rot, qpos, idx, ks_all, vs_all)
    lead = ((0, 0), (WINDOW - win_k.shape[1], 0), (0, 0), (0, 0))
    o_win = window_attend(q_rot, jnp.pad(win_k, lead), jnp.pad(win_v, lead), kw, vw, pos0)
    g = jax.nn.sigmoid(g_nsa.astype(jnp.float32)).astype(dt).reshape(B, L, 3, N_HEADS, 1)
    o_attn = (g[:, :, 0] * o_cmp + g[:, :, 1] * o_slc + g[:, :, 2] * o_win).reshape(B, L, Q_WIDTH)
    o_pool, pool_new = pool_mixer(u_pool, pool_prev, pos0, p['pool_grp_w'], p['pool_scale'])
    gate = jax.nn.sigmoid(g_br.astype(jnp.float32)).astype(dt)
    merged = gate[..., :D_MODEL] * (o_pool @ p['w_branch_pool']) + gate[..., D_MODEL:] * (o_attn @ p['w_branch_attn'])
    h = layer_norm(DN_ALPHA * x + merged @ p['w_out'], p['ln1_g'], p['ln1_b'])
    z = peer(h, p['peer_wq'], p['peer_subkeys'], p['peer_u'], p['peer_v'])
    y = layer_norm(DN_ALPHA * h + z, p['ln2_g'], p['ln2_b'])
    win_k_new = jnp.concatenate([win_k, kw], axis=1)[:, -win_rows:]
    win_v_new = jnp.concatenate([win_v, vw], axis=1)[:, -win_rows:]
    return y, (kc, vc, ks, vs, win_k_new, win_v_new, pool_new)


def kernel(x_prompt, x_sample, cache_k_cmp, cache_v_cmp, cache_k_slc, cache_v_slc, cache_k_win, cache_v_win, state_pool, page_table, w_in, pool_grp_w, pool_scale, cmp_pe_k, cmp_w1_k, cmp_w2_k, cmp_pe_v, cmp_w1_v, cmp_w2_v, w_branch_pool, w_branch_attn, w_out, ln1_g, ln1_b, peer_wq, peer_subkeys, peer_u, peer_v, ln2_g, ln2_b):
    B, L = x_prompt.shape[:2]
    zero_win = jnp.zeros((B, 0, N_KV, HEAD_DIM), x_prompt.dtype)
    zero_pool = jnp.zeros((B, POOL_STATE, POOL_WIDTH), x_prompt.dtype)

    def gather_pages(c):
        g = c[page_table]
        return g.reshape(g.shape[0], -1, N_KV, HEAD_DIM)

    hp, hs = x_prompt, x_sample
    st_prompt, st_sample = [], []
    for l in range(DEPTH):
        p = {'w_in': w_in[l], 'pool_grp_w': pool_grp_w[l], 'pool_scale': pool_scale[l],
             'cmp_pe_k': cmp_pe_k[l], 'cmp_w1_k': cmp_w1_k[l], 'cmp_w2_k': cmp_w2_k[l],
             'cmp_pe_v': cmp_pe_v[l], 'cmp_w1_v': cmp_w1_v[l], 'cmp_w2_v': cmp_w2_v[l],
             'w_branch_pool': w_branch_pool[l], 'w_branch_attn': w_branch_attn[l], 'w_out': w_out[l],
             'ln1_g': ln1_g[l], 'ln1_b': ln1_b[l], 'peer_wq': peer_wq[l], 'peer_subkeys': peer_subkeys[l],
             'peer_u': peer_u[l], 'peer_v': peer_v[l], 'ln2_g': ln2_g[l], 'ln2_b': ln2_b[l]}
        hp, sp = decoder_layer(hp, 0, None, zero_win, zero_win, zero_pool, min(WINDOW, L), p)
        past = (gather_pages(cache_k_cmp[l]), gather_pages(cache_v_cmp[l]),
                gather_pages(cache_k_slc[l]), gather_pages(cache_v_slc[l]))
        hs, ss = decoder_layer(hs, PAST_LEN, past, cache_k_win[l], cache_v_win[l], state_pool[l],
                               cache_k_win.shape[2], p)
        st_prompt.append(sp)
        st_sample.append(ss)

    def stack(states, i):
        return jnp.stack([s[i] for s in states], axis=0)

    return (hp, hs) + tuple(stack(st_prompt, i) for i in range(7)) + tuple(stack(st_sample, i) for i in range(7))
```

```python
import functools

import numpy as np
import jax
import jax.numpy as jnp
from jax import lax
from jax.experimental import pallas as pl
from jax.experimental.pallas import tpu as pltpu


D_MODEL = 2048
BATCH = 8
SEQ = 2048
DEPTH = 1
DEC_BATCH = 128
DEC_SEQ = 4
PAST_LEN = 2048
PAGE_SIZE = 128

POOL_WINDOWS = (2, 4, 8, 16)
N_POOL_GROUPS = len(POOL_WINDOWS)
POOL_WIDTH = D_MODEL // 2
POOL_GROUP = POOL_WIDTH // N_POOL_GROUPS
POOL_STATE = max(POOL_WINDOWS) - 1
N_HEADS = 16
HEAD_DIM = 128
N_KV = 4
HPG = N_HEADS // N_KV
ROT_DIM = HEAD_DIM // 4
ROPE_THETA = 500000.0
ATTN_SCALE = HEAD_DIM ** -0.5
CMP_LEN = 32
CMP_STRIDE = 16
CMP_RATIO = CMP_LEN // CMP_STRIDE
SLC_BLK = 64
N_SEL = 16
WINDOW = 512
WIN_QBLK = 128
SLC_QCHUNK = 64
PEER_HEADS = 8
N_KEYS = 128
N_EXPERTS = N_KEYS * N_KEYS
PEER_TOPK = 16
PEER_QDIM = 256
PEER_HALF = PEER_QDIM // 2
PEER_BLOCK = 64
DN_ALPHA = (2.0 * DEPTH) ** 0.25
DN_BETA = (8.0 * DEPTH) ** -0.25
LN_EPS = 1e-5

Q_WIDTH = N_HEADS * HEAD_DIM
KV_WIDTH = N_KV * HEAD_DIM
IN_SIZES = (POOL_WIDTH, Q_WIDTH, KV_WIDTH, KV_WIDTH, KV_WIDTH, KV_WIDTH, KV_WIDTH, KV_WIDTH, 3 * N_HEADS, 2 * D_MODEL)

V7X_LANES = 128


def _matmul_kernel(a_ref, b_ref, o_ref):
    o_ref[...] = jnp.dot(a_ref[...], b_ref[...], preferred_element_type=jnp.float32)


def matmul_bf16(a, b, tm, tn):
    M, K = a.shape
    _, N = b.shape
    tm, tn = min(tm, M), min(tn, N)
    assert M % tm == 0 and N % tn == 0
    vmem = 2 * (tm * K * 2 + K * tn * 2 + tm * tn * 4) + (4 << 20)
    return pl.pallas_call(
        _matmul_kernel,
        grid=(M // tm, N // tn),
        in_specs=[pl.BlockSpec((tm, K), lambda i, j: (i, 0)),
                  pl.BlockSpec((K, tn), lambda i, j: (0, j))],
        out_specs=pl.BlockSpec((tm, tn), lambda i, j: (i, j)),
        out_shape=jax.ShapeDtypeStruct((M, N), jnp.float32),
        compiler_params=pltpu.CompilerParams(
            dimension_semantics=("parallel", "arbitrary"), vmem_limit_bytes=vmem),
        name="matmul_bf16",
    )(a, b)


def input_projection(x, w_in):
    B, L, D = x.shape
    xb = x.reshape(B * L, D).astype(jnp.bfloat16)
    g0 = sum(IN_SIZES[:8])
    g1 = g0 + IN_SIZES[8]
    w_main = jnp.concatenate([w_in[:, :g0], w_in[:, g1:]], axis=1).astype(jnp.bfloat16)
    w_gate = jnp.pad(w_in[:, g0:g1], ((0, 0), (0, V7X_LANES - IN_SIZES[8]))).astype(jnp.bfloat16)
    main = matmul_bf16(xb, w_main, 1024, 512)
    gate = matmul_bf16(xb, w_gate, 1024, V7X_LANES)
    proj = jnp.concatenate([main[:, :g0], gate[:, :IN_SIZES[8]], main[:, g0:]], axis=1)
    return proj.reshape(B, L, -1)


ATTN_TQ = 128
ATTN_TK = 256
MASK_BIAS = -30000.0
NEG_BIG = -1e30


def _attn_kernel(*refs, tq, tk, window, n_blk):
    if n_blk:
        q_ref, k_ref, v_ref, sb_ref, o_ref, m_sc, l_sc, acc_sc = refs
    else:
        q_ref, k_ref, v_ref, o_ref, m_sc, l_sc, acc_sc = refs
    q0 = pl.program_id(2) * tq
    rows = HPG * tq
    q = (q_ref[0] * ATTN_SCALE).astype(jnp.bfloat16).reshape(rows, HEAD_DIM)
    m_sc[...] = jnp.full(m_sc.shape, NEG_BIG, jnp.float32)
    l_sc[...] = jnp.zeros(l_sc.shape, jnp.float32)
    acc_sc[...] = jnp.zeros(acc_sc.shape, jnp.float32)
    qpos = q0 + lax.broadcasted_iota(jnp.int32, (tq, tk), 0)
    hi = (q0 + tq - 1) // tk
    lo = 0 if window is None else jnp.maximum(q0 - (window - 1), 0) // tk

    def body(kt, carry):
        k0 = pl.multiple_of(kt * tk, tk)
        kb = k_ref[0, pl.ds(k0, tk), :].astype(jnp.bfloat16)
        vb = v_ref[0, pl.ds(k0, tk), :].astype(jnp.bfloat16)
        s = lax.dot_general(q, kb, (((1,), (1,)), ((), ())), preferred_element_type=jnp.float32)
        kpos = k0 + lax.broadcasted_iota(jnp.int32, (tq, tk), 1)
        ok = kpos <= qpos
        if window is not None:
            ok = ok & (kpos > qpos - window)
        if n_blk:
            kblk = (k0 + lax.broadcasted_iota(jnp.int32, (n_blk, tk), 1)) // SLC_BLK
            expand = (kblk == lax.broadcasted_iota(jnp.int32, (n_blk, tk), 0)).astype(jnp.bfloat16)
            bias = jnp.where(ok, jnp.dot(sb_ref[0, 0], expand, preferred_element_type=jnp.float32), NEG_BIG)
        else:
            bias = jnp.where(ok, 0.0, NEG_BIG)
        s = (s.reshape(HPG, tq, tk) + bias[None]).reshape(rows, tk)
        m_old = m_sc[...]
        m_new = jnp.maximum(m_old, jnp.max(s, axis=-1, keepdims=True))
        alpha = jnp.exp(m_old - m_new)
        p = jnp.exp(s - m_new)
        l_sc[...] = alpha * l_sc[...] + jnp.sum(p, axis=-1, keepdims=True)
        acc_sc[...] = alpha * acc_sc[...] + jnp.dot(p.astype(jnp.bfloat16), vb, preferred_element_type=jnp.float32)
        m_sc[...] = m_new
        return carry

    lax.fori_loop(lo, hi + 1, body, 0)
    o_ref[0] = (acc_sc[...] / l_sc[...]).reshape(HPG, tq, HEAD_DIM)


def prompt_attention(q, k, v, sel_bias, window):
    B, L = q.shape[:2]
    tq, tk = min(ATTN_TQ, L), min(ATTN_TK, L)
    assert L % tq == 0 and L % tk == 0
    qh = q.transpose(0, 2, 1, 3)
    kf = k.reshape(B, L, KV_WIDTH)
    vf = v.reshape(B, L, KV_WIDTH)
    n_blk = 0 if sel_bias is None else sel_bias.shape[-1]
    in_specs = [pl.BlockSpec((1, HPG, tq, HEAD_DIM), lambda b, g, i: (b, g, i, 0)),
                pl.BlockSpec((1, L, HEAD_DIM), lambda b, g, i: (b, 0, g)),
                pl.BlockSpec((1, L, HEAD_DIM), lambda b, g, i: (b, 0, g))]
    args = [qh, kf, vf]
    if n_blk:
        in_specs.append(pl.BlockSpec((1, 1, tq, n_blk), lambda b, g, i: (b, g, i, 0)))
        args.append(sel_bias)
    rows = HPG * tq
    o = pl.pallas_call(
        functools.partial(_attn_kernel, tq=tq, tk=tk, window=window, n_blk=n_blk),
        grid=(B, N_KV, L // tq),
        in_specs=in_specs,
        out_specs=pl.BlockSpec((1, HPG, tq, HEAD_DIM), lambda b, g, i: (b, g, i, 0)),
        out_shape=jax.ShapeDtypeStruct((B, N_HEADS, L, HEAD_DIM), jnp.float32),
        scratch_shapes=[pltpu.VMEM((rows, 1), jnp.float32), pltpu.VMEM((rows, 1), jnp.float32),
                        pltpu.VMEM((rows, HEAD_DIM), jnp.float32)],
        compiler_params=pltpu.CompilerParams(
            dimension_semantics=("parallel", "parallel", "arbitrary"), vmem_limit_bytes=32 << 20),
        name="slc_attention" if n_blk else "win_attention",
    )(*args)
    return o.transpose(0, 2, 1, 3)


def selection_bias(idx, n_slc):
    sel = jnp.any(idx[..., None] == jnp.arange(n_slc, dtype=idx.dtype), axis=-2)
    return jnp.where(sel, 0.0, MASK_BIAS).astype(jnp.bfloat16)


def dense(a, w):
    lead = a.shape[:-1]
    out = matmul_bf16(a.reshape(-1, a.shape[-1]).astype(jnp.bfloat16), w.astype(jnp.bfloat16), 1024, 512)
    return out.reshape(lead + (w.shape[1],))


def layer_norm(x, g, b):
    xf = x.astype(jnp.float32)
    mu = jnp.mean(xf, axis=-1, keepdims=True)
    var = jnp.mean(jnp.square(xf - mu), axis=-1, keepdims=True)
    return ((xf - mu) * lax.rsqrt(var + LN_EPS) * g + b).astype(x.dtype)


def masked_softmax(s, mask):
    s = jnp.where(mask, s.astype(jnp.float32), -jnp.inf)
    m = jnp.max(s, axis=-1, keepdims=True)
    e = jnp.exp(s - jnp.where(jnp.isfinite(m), m, 0.0))
    den = jnp.sum(e, axis=-1, keepdims=True)
    return e / jnp.where(den > 0.0, den, 1.0)


def rope(x, pos):
    half = ROT_DIM // 2
    inv = ROPE_THETA ** (-jnp.arange(half, dtype=jnp.float32) / half)
    ang = pos.astype(jnp.float32)[:, None] * inv
    cos = jnp.cos(ang)[:, None, :]
    sin = jnp.sin(ang)[:, None, :]
    xr = x[..., :ROT_DIM].astype(jnp.float32)
    x1, x2 = xr[..., :half], xr[..., half:]
    rot = jnp.concatenate([x1 * cos - x2 * sin, x2 * cos + x1 * sin], axis=-1).astype(x.dtype)
    return jnp.concatenate([rot, x[..., ROT_DIM:]], axis=-1)


def pool_mixer(u, u_prev, pos0, w_grp, scale):
    B, L, P = u.shape
    ext = jnp.concatenate([u_prev, u], axis=1)
    cs = jnp.pad(jnp.cumsum(ext.astype(jnp.float32), axis=1), ((0, 0), (1, 0), (0, 0)))
    pos = pos0 + np.arange(L)
    upto = cs[:, POOL_STATE + 1:]
    means = []
    for gi, w in enumerate(POOL_WINDOWS):
        c = slice(gi * POOL_GROUP, (gi + 1) * POOL_GROUP)
        before = cs[:, POOL_STATE + 1 - w:POOL_STATE + 1 - w + L, c]
        cnt = np.minimum(w, pos + 1).astype(np.float32)[None, :, None]
        means.append((upto[..., c] - before) / cnt)
    d = (jnp.concatenate(means, axis=-1) - u.astype(jnp.float32)).astype(u.dtype)
    d = d.reshape(B, L, N_POOL_GROUPS, POOL_GROUP)
    y = jnp.einsum('blgc,gce->blge', d, w_grp).reshape(B, L, P) * scale
    return y, ext[:, -POOL_STATE:]


def compress(kv, pe, w1, w2):
    B, Lt = kv.shape[:2]
    n_chunk = Lt // CMP_STRIDE
    n_cmp = n_chunk - CMP_RATIO + 1
    ch = kv[:, :n_chunk * CMP_STRIDE].reshape(B, n_chunk, CMP_STRIDE, N_KV, HEAD_DIM)
    w1r = w1.reshape(CMP_RATIO, CMP_STRIDE, HEAD_DIM, HEAD_DIM)
    a = jnp.einsum('bcsgd,rsde->rbcge', ch, w1r)
    hid = jnp.einsum('jd,jde->e', pe, w1)
    for r in range(CMP_RATIO):
        hid = hid + a[r, :, r:r + n_cmp]
    return jnp.einsum('bnge,ef->bngf', jax.nn.gelu(hid, approximate=False), w2)


def cmp_attend(q, qpos, kcb, vcb):
    B, L = q.shape[:2]
    n_cmp = kcb.shape[1]
    qg = q.reshape(B, L, N_KV, HPG, HEAD_DIM)
    s = jnp.einsum('blghd,bngd->bghln', qg, kcb) * ATTN_SCALE
    ends = np.arange(n_cmp) * CMP_STRIDE + CMP_LEN - 1
    pr = masked_softmax(s, ends[None, :] <= qpos[:, None])
    o = jnp.einsum('bghln,bngd->blghd', pr.astype(vcb.dtype), vcb).reshape(B, L, N_HEADS, HEAD_DIM)
    return o, jnp.sum(pr, axis=2)


def select_blocks(imp_c, qpos, Lt):
    n_cmp = imp_c.shape[-1]
    n_slc = -(-Lt // SLC_BLK)
    cs = np.arange(n_cmp) * CMP_STRIDE
    bs = np.arange(n_slc) * SLC_BLK
    inter = ((cs[:, None] < bs[None, :] + SLC_BLK) & (cs[:, None] + CMP_LEN > bs[None, :])).astype(np.float32)
    imp = jnp.einsum('bgln,nj->bglj', imp_c, jnp.asarray(inter))
    cur = (qpos // SLC_BLK)[:, None]
    j = np.arange(n_slc)[None, :]
    valid = bs[None, :] <= qpos[:, None]
    forced = valid & ((j == 0) | (j == cur) | (j == cur - 1))
    score = jnp.where(forced, jnp.inf, jnp.where(valid, imp, -jnp.inf))
    _, idx = lax.top_k(score, min(N_SEL, n_slc))
    return idx


def slc_attend(q, qpos, idx, k, v):
    B, L = q.shape[:2]
    Lt = k.shape[1]
    n_slc = -(-Lt // SLC_BLK)
    padk = ((0, 0), (0, n_slc * SLC_BLK - Lt), (0, 0), (0, 0))

    def blocks(a):
        return jnp.pad(a, padk).reshape(B, n_slc, SLC_BLK, N_KV, HEAD_DIM).transpose(0, 3, 1, 2, 4)

    kb, vb = blocks(k), blocks(v)
    n_q = B * L
    n_pad = (-n_q) % SLC_QCHUNK

    def flat(a):
        a = jnp.pad(a, [(0, n_pad)] + [(0, 0)] * (a.ndim - 1))
        return a.reshape((-1, SLC_QCHUNK) + a.shape[1:])

    qf = flat(q.reshape(n_q, N_KV, HPG, HEAD_DIM))
    idf = flat(idx.transpose(0, 2, 1, 3).reshape(n_q, N_KV, -1))
    sid = flat(jnp.repeat(jnp.arange(B, dtype=jnp.int32), L))
    pf = flat(jnp.asarray(np.broadcast_to(qpos, (B, L)).reshape(-1), jnp.int32))
    g_ix = jnp.arange(N_KV)[None, :, None]

    def step(args):
        qc, ic, sc, pc = args
        C, _, n_sel = ic.shape
        kg = kb[sc[:, None, None], g_ix, ic]
        vg = vb[sc[:, None, None], g_ix, ic]
        s = jnp.einsum('cghd,cgnkd->cghnk', qc, kg).reshape(C, N_KV, HPG, n_sel * SLC_BLK) * ATTN_SCALE
        kpos = ic[..., None] * SLC_BLK + jnp.arange(SLC_BLK)
        mask = (kpos <= pc[:, None, None, None]).reshape(C, N_KV, 1, n_sel * SLC_BLK)
        pr = masked_softmax(s, mask)
        vflat = vg.reshape(C, N_KV, n_sel * SLC_BLK, HEAD_DIM)
        return jnp.einsum('cghm,cgmd->cghd', pr.astype(vg.dtype), vflat)

    o = lax.map(step, (qf, idf, sid, pf))
    return o.reshape(-1, N_HEADS, HEAD_DIM)[:n_q].reshape(B, L, N_HEADS, HEAD_DIM)


def window_attend(q, k_lead, v_lead, k_new, v_new, pos0):
    B, L = q.shape[:2]
    kf = jnp.concatenate([k_lead, k_new], axis=1)
    vf = jnp.concatenate([v_lead, v_new], axis=1)
    kpos = jnp.asarray(pos0 - WINDOW + np.arange(WINDOW + L), jnp.int32)
    qb = WIN_QBLK if L % WIN_QBLK == 0 else L
    nqb = L // qb
    qblocks = q.reshape(B, nqb, qb, N_KV, HPG, HEAD_DIM).transpose(1, 0, 2, 3, 4, 5)

    def step(args):
        b, qc = args
        start = b * qb
        kc = lax.dynamic_slice_in_dim(kf, start, WINDOW + qb, axis=1)
        vc = lax.dynamic_slice_in_dim(vf, start, WINDOW + qb, axis=1)
        kp = lax.dynamic_slice_in_dim(kpos, start, WINDOW + qb, axis=0)[None, :]
        qp = (pos0 + start + jnp.arange(qb))[:, None]
        s = jnp.einsum('bighd,bkgd->bghik', qc, kc) * ATTN_SCALE
        mask = (kp <= qp) & (kp > qp - WINDOW) & (kp >= 0)
        pr = masked_softmax(s, mask)
        return jnp.einsum('bghik,bkgd->bighd', pr.astype(vc.dtype), vc)

    o = lax.map(step, (jnp.arange(nqb, dtype=jnp.int32), qblocks))
    return o.transpose(1, 0, 2, 3, 4, 5).reshape(B, L, N_HEADS, HEAD_DIM)


def peer(h, wq, sub_keys, u_tab, v_tab):
    B, L, D = h.shape
    t = h.reshape(B * L, D)
    T = t.shape[0]
    q = dense(t, wq).reshape(T, PEER_HEADS, 2, PEER_HALF)
    s = jnp.einsum('thpc,hpkc->thpk', q, sub_keys).astype(jnp.float32)
    s1, i1 = lax.top_k(s[:, :, 0], PEER_TOPK)
    s2, i2 = lax.top_k(s[:, :, 1], PEER_TOPK)
    cand = (s1[..., :, None] + s2[..., None, :]).reshape(T, PEER_HEADS, PEER_TOPK * PEER_TOPK)
    cid = (i1[..., :, None] * N_KEYS + i2[..., None, :]).reshape(T, PEER_HEADS, PEER_TOPK * PEER_TOPK)
    top, sel = lax.top_k(cand, PEER_TOPK)
    eid = jnp.take_along_axis(cid, sel, axis=-1)
    gate = jax.nn.softmax(top, axis=-1)
    n_pad = (-T) % PEER_BLOCK

    def chunks(a):
        a = jnp.pad(a, [(0, n_pad)] + [(0, 0)] * (a.ndim - 1))
        return a.reshape((-1, PEER_BLOCK) + a.shape[1:])

    def step(args):
        tc, ec, gc = args
        act = jax.nn.gelu(jnp.einsum('cd,chkd->chk', tc, u_tab[ec]).astype(jnp.float32), approximate=False)
        return jnp.einsum('chk,chkd->cd', (gc * act).astype(tc.dtype), v_tab[ec])

    out = lax.map(step, (chunks(t), chunks(eid), chunks(gate)))
    return out.reshape(-1, D)[:T].reshape(B, L, D)


def decoder_layer(x, pos0, past, win_k, win_v, pool_prev, win_rows, p):
    B, L, _ = x.shape
    dt = x.dtype
    qpos = pos0 + np.arange(L)
    qpos_j = jnp.asarray(qpos, jnp.int32)
    proj = input_projection(x, p['w_in'])
    splits = np.cumsum(IN_SIZES)[:-1].tolist()
    u_pool, q, kc, vc, ks, vs, kw, vw, g_nsa, g_br = jnp.split(proj, splits, axis=-1)
    q = q.reshape(B, L, N_HEADS, HEAD_DIM)
    kc, vc, ks, vs, kw, vw = [a.reshape(B, L, N_KV, HEAD_DIM) for a in (kc, vc, ks, vs, kw, vw)]
    q_rot = rope(q, qpos_j)
    ks = rope(ks, qpos_j)
    kw = rope(kw, qpos_j)
    if past is None:
        kc_all, vc_all, ks_all, vs_all = kc, vc, ks, vs
    else:
        kc_all, vc_all, ks_all, vs_all = [jnp.concatenate([a, b], axis=1) for a, b in zip(past, (kc, vc, ks, vs))]
    kcb = compress(kc_all, p['cmp_pe_k'], p['cmp_w1_k'], p['cmp_w2_k'])
    vcb = compress(vc_all, p['cmp_pe_v'], p['cmp_w1_v'], p['cmp_w2_v'])
    o_cmp, imp = cmp_attend(q, qpos, kcb, vcb)
    idx = select_blocks(imp, qpos, ks_all.shape[1])
    if past is None:
        o_slc = prompt_attention(q_rot, ks, vs, selection_bias(idx, -(-L // SLC_BLK)), None)
        o_win = prompt_attention(q_rot, kw, vw, None, WINDOW)
    else:
        o_slc = slc_attend(q_rot, qpos, idx, ks_all, vs_all)
        lead = ((0, 0), (WINDOW - win_k.shape[1], 0), (0, 0), (0, 0))
        o_win = window_attend(q_rot, jnp.pad(win_k, lead), jnp.pad(win_v, lead), kw, vw, pos0)
    g = jax.nn.sigmoid(g_nsa.astype(jnp.float32)).astype(dt).reshape(B, L, 3, N_HEADS, 1)
    o_attn = (g[:, :, 0] * o_cmp + g[:, :, 1] * o_slc + g[:, :, 2] * o_win).reshape(B, L, Q_WIDTH)
    o_pool, pool_new = pool_mixer(u_pool, pool_prev, pos0, p['pool_grp_w'], p['pool_scale'])
    gate = jax.nn.sigmoid(g_br.astype(jnp.float32)).astype(dt)
    merged = (gate[..., :D_MODEL] * dense(o_pool, p['w_branch_pool'])
              + gate[..., D_MODEL:] * dense(o_attn, p['w_branch_attn']))
    h = layer_norm(DN_ALPHA * x + dense(merged, p['w_out']), p['ln1_g'], p['ln1_b'])
    z = peer(h, p['peer_wq'], p['peer_subkeys'], p['peer_u'], p['peer_v'])
    y = layer_norm(DN_ALPHA * h + z, p['ln2_g'], p['ln2_b'])
    win_k_new = jnp.concatenate([win_k, kw], axis=1)[:, -win_rows:]
    win_v_new = jnp.concatenate([win_v, vw], axis=1)[:, -win_rows:]
    return y, (kc, vc, ks, vs, win_k_new, win_v_new, pool_new)


def kernel(x_prompt, x_sample, cache_k_cmp, cache_v_cmp, cache_k_slc, cache_v_slc, cache_k_win, cache_v_win, state_pool, page_table, w_in, pool_grp_w, pool_scale, cmp_pe_k, cmp_w1_k, cmp_w2_k, cmp_pe_v, cmp_w1_v, cmp_w2_v, w_branch_pool, w_branch_attn, w_out, ln1_g, ln1_b, peer_wq, peer_subkeys, peer_u, peer_v, ln2_g, ln2_b):
    B, L = x_prompt.shape[:2]
    zero_win = jnp.zeros((B, 0, N_KV, HEAD_DIM), x_prompt.dtype)
    zero_pool = jnp.zeros((B, POOL_STATE, POOL_WIDTH), x_prompt.dtype)

    def gather_pages(c):
        g = c[page_table]
        return g.reshape(g.shape[0], -1, N_KV, HEAD_DIM)

    hp, hs = x_prompt, x_sample
    st_prompt, st_sample = [], []
    for l in range(DEPTH):
        p = {'w_in': w_in[l], 'pool_grp_w': pool_grp_w[l], 'pool_scale': pool_scale[l],
             'cmp_pe_k': cmp_pe_k[l], 'cmp_w1_k': cmp_w1_k[l], 'cmp_w2_k': cmp_w2_k[l],
             'cmp_pe_v': cmp_pe_v[l], 'cmp_w1_v': cmp_w1_v[l], 'cmp_w2_v': cmp_w2_v[l],
             'w_branch_pool': w_branch_pool[l], 'w_branch_attn': w_branch_attn[l], 'w_out': w_out[l],
             'ln1_g': ln1_g[l], 'ln1_b': ln1_b[l], 'peer_wq': peer_wq[l], 'peer_subkeys': peer_subkeys[l],
             'peer_u': peer_u[l], 'peer_v': peer_v[l], 'ln2_g': ln2_g[l], 'ln2_b': ln2_b[l]}
        hp, sp = decoder_layer(hp, 0, None, zero_win, zero_win, zero_pool, min(WINDOW, L), p)
        past = (gather_pages(cache_k_cmp[l]), gather_pages(cache_v_cmp[l]),
                gather_pages(cache_k_slc[l]), gather_pages(cache_v_slc[l]))
        hs, ss = decoder_layer(hs, PAST_LEN, past, cache_k_win[l], cache_v_win[l], state_pool[l],
                               cache_k_win.shape[2], p)
        st_prompt.append(sp)
        st_sample.append(ss)

    def stack(states, i):
        return jnp.stack([s[i] for s in states], axis=0)

    return (hp, hs) + tuple(stack(st_prompt, i) for i in range(7)) + tuple(stack(st_sample, i) for i in range(7))
```

```python
import functools

import numpy as np
import jax
import jax.numpy as jnp
from jax import lax
from jax.experimental import pallas as pl
from jax.experimental.pallas import tpu as pltpu


D_MODEL = 2048
BATCH = 8
SEQ = 2048
DEPTH = 1
DEC_BATCH = 128
DEC_SEQ = 4
PAST_LEN = 2048
PAGE_SIZE = 128

POOL_WINDOWS = (2, 4, 8, 16)
N_POOL_GROUPS = len(POOL_WINDOWS)
POOL_WIDTH = D_MODEL // 2
POOL_GROUP = POOL_WIDTH // N_POOL_GROUPS
POOL_STATE = max(POOL_WINDOWS) - 1
N_HEADS = 16
HEAD_DIM = 128
N_KV = 4
HPG = N_HEADS // N_KV
ROT_DIM = HEAD_DIM // 4
ROPE_THETA = 500000.0
ATTN_SCALE = HEAD_DIM ** -0.5
CMP_LEN = 32
CMP_STRIDE = 16
CMP_RATIO = CMP_LEN // CMP_STRIDE
SLC_BLK = 64
N_SEL = 16
WINDOW = 512
WIN_QBLK = 128
SLC_QCHUNK = 64
PEER_HEADS = 8
N_KEYS = 128
N_EXPERTS = N_KEYS * N_KEYS
PEER_TOPK = 16
PEER_QDIM = 256
PEER_HALF = PEER_QDIM // 2
PEER_BLOCK = 64
DN_ALPHA = (2.0 * DEPTH) ** 0.25
DN_BETA = (8.0 * DEPTH) ** -0.25
LN_EPS = 1e-5

Q_WIDTH = N_HEADS * HEAD_DIM
KV_WIDTH = N_KV * HEAD_DIM
IN_SIZES = (POOL_WIDTH, Q_WIDTH, KV_WIDTH, KV_WIDTH, KV_WIDTH, KV_WIDTH, KV_WIDTH, KV_WIDTH, 3 * N_HEADS, 2 * D_MODEL)

V7X_LANES = 128


def _matmul_kernel(a_ref, b_ref, o_ref):
    o_ref[...] = jnp.dot(a_ref[...], b_ref[...], preferred_element_type=jnp.float32)


def matmul_bf16(a, b, tm, tn):
    M, K = a.shape
    _, N = b.shape
    tm, tn = min(tm, M), min(tn, N)
    assert M % tm == 0 and N % tn == 0
    vmem = 2 * (tm * K * 2 + K * tn * 2 + tm * tn * 4) + (4 << 20)
    return pl.pallas_call(
        _matmul_kernel,
        grid=(M // tm, N // tn),
        in_specs=[pl.BlockSpec((tm, K), lambda i, j: (i, 0)),
                  pl.BlockSpec((K, tn), lambda i, j: (0, j))],
        out_specs=pl.BlockSpec((tm, tn), lambda i, j: (i, j)),
        out_shape=jax.ShapeDtypeStruct((M, N), jnp.float32),
        compiler_params=pltpu.CompilerParams(
            dimension_semantics=("parallel", "arbitrary"), vmem_limit_bytes=vmem),
        name="matmul_bf16",
    )(a, b)


def input_projection(x, w_in):
    B, L, D = x.shape
    xb = x.reshape(B * L, D).astype(jnp.bfloat16)
    g0 = sum(IN_SIZES[:8])
    g1 = g0 + IN_SIZES[8]
    w_main = jnp.concatenate([w_in[:, :g0], w_in[:, g1:]], axis=1).astype(jnp.bfloat16)
    w_gate = jnp.pad(w_in[:, g0:g1], ((0, 0), (0, V7X_LANES - IN_SIZES[8]))).astype(jnp.bfloat16)
    main = matmul_bf16(xb, w_main, 1024, 512)
    gate = matmul_bf16(xb, w_gate, 1024, V7X_LANES)
    proj = jnp.concatenate([main[:, :g0], gate[:, :IN_SIZES[8]], main[:, g0:]], axis=1)
    return proj.reshape(B, L, -1)


ATTN_TQ = 128
ATTN_TK = 256
MASK_BIAS = -30000.0
NEG_BIG = -1e30


def _attn_kernel(*refs, tq, tk, window, n_blk):
    if n_blk:
        q_ref, k_ref, v_ref, sb_ref, o_ref, m_sc, l_sc, acc_sc = refs
    else:
        q_ref, k_ref, v_ref, o_ref, m_sc, l_sc, acc_sc = refs
    q0 = pl.program_id(2) * tq
    rows = HPG * tq
    q = (q_ref[0] * ATTN_SCALE).astype(jnp.bfloat16).reshape(rows, HEAD_DIM)
    m_sc[...] = jnp.full(m_sc.shape, NEG_BIG, jnp.float32)
    l_sc[...] = jnp.zeros(l_sc.shape, jnp.float32)
    acc_sc[...] = jnp.zeros(acc_sc.shape, jnp.float32)
    qpos = q0 + lax.broadcasted_iota(jnp.int32, (tq, tk), 0)
    hi = (q0 + tq - 1) // tk
    lo = 0 if window is None else jnp.maximum(q0 - (window - 1), 0) // tk

    def body(kt, carry):
        k0 = pl.multiple_of(kt * tk, tk)
        kb = k_ref[0, pl.ds(k0, tk), :].astype(jnp.bfloat16)
        vb = v_ref[0, pl.ds(k0, tk), :].astype(jnp.bfloat16)
        s = lax.dot_general(q, kb, (((1,), (1,)), ((), ())), preferred_element_type=jnp.float32)
        kpos = k0 + lax.broadcasted_iota(jnp.int32, (tq, tk), 1)
        ok = kpos <= qpos
        if window is not None:
            ok = ok & (kpos > qpos - window)
        if n_blk:
            kblk = (k0 + lax.broadcasted_iota(jnp.int32, (n_blk, tk), 1)) // SLC_BLK
            expand = (kblk == lax.broadcasted_iota(jnp.int32, (n_blk, tk), 0)).astype(jnp.bfloat16)
            bias = jnp.where(ok, jnp.dot(sb_ref[0, 0], expand, preferred_element_type=jnp.float32), NEG_BIG)
        else:
            bias = jnp.where(ok, 0.0, NEG_BIG)
        s = (s.reshape(HPG, tq, tk) + bias[None]).reshape(rows, tk)
        m_old = m_sc[...]
        m_new = jnp.maximum(m_old, jnp.max(s, axis=-1, keepdims=True))
        alpha = jnp.exp(m_old - m_new)
        p = jnp.exp(s - m_new)
        l_sc[...] = alpha * l_sc[...] + jnp.sum(p, axis=-1, keepdims=True)
        acc_sc[...] = alpha * acc_sc[...] + jnp.dot(p.astype(jnp.bfloat16), vb, preferred_element_type=jnp.float32)
        m_sc[...] = m_new
        return carry

    lax.fori_loop(lo, hi + 1, body, 0)
    o_ref[0] = (acc_sc[...] / l_sc[...]).reshape(HPG, tq, HEAD_DIM)


def prompt_attention(q, k, v, sel_bias, window):
    B, L = q.shape[:2]
    tq, tk = min(ATTN_TQ, L), min(ATTN_TK, L)
    assert L % tq == 0 and L % tk == 0
    qh = q.transpose(0, 2, 1, 3)
    kf = k.reshape(B, L, KV_WIDTH)
    vf = v.reshape(B, L, KV_WIDTH)
    n_blk = 0 if sel_bias is None else sel_bias.shape[-1]
    in_specs = [pl.BlockSpec((1, HPG, tq, HEAD_DIM), lambda b, g, i: (b, g, i, 0)),
                pl.BlockSpec((1, L, HEAD_DIM), lambda b, g, i: (b, 0, g)),
                pl.BlockSpec((1, L, HEAD_DIM), lambda b, g, i: (b, 0, g))]
    args = [qh, kf, vf]
    if n_blk:
        in_specs.append(pl.BlockSpec((1, 1, tq, n_blk), lambda b, g, i: (b, g, i, 0)))
        args.append(sel_bias)
    rows = HPG * tq
    o = pl.pallas_call(
        functools.partial(_attn_kernel, tq=tq, tk=tk, window=window, n_blk=n_blk),
        grid=(B, N_KV, L // tq),
        in_specs=in_specs,
        out_specs=pl.BlockSpec((1, HPG, tq, HEAD_DIM), lambda b, g, i: (b, g, i, 0)),
        out_shape=jax.ShapeDtypeStruct((B, N_HEADS, L, HEAD_DIM), jnp.float32),
        scratch_shapes=[pltpu.VMEM((rows, 1), jnp.float32), pltpu.VMEM((rows, 1), jnp.float32),
                        pltpu.VMEM((rows, HEAD_DIM), jnp.float32)],
        compiler_params=pltpu.CompilerParams(
            dimension_semantics=("parallel", "parallel", "arbitrary"), vmem_limit_bytes=32 << 20),
        name="slc_attention" if n_blk else "win_attention",
    )(*args)
    return o.transpose(0, 2, 1, 3)


def selection_bias(idx, n_slc):
    sel = jnp.any(idx[..., None] == jnp.arange(n_slc, dtype=idx.dtype), axis=-2)
    return jnp.where(sel, 0.0, MASK_BIAS).astype(jnp.bfloat16)


def dense(a, w):
    lead = a.shape[:-1]
    out = matmul_bf16(a.reshape(-1, a.shape[-1]).astype(jnp.bfloat16), w.astype(jnp.bfloat16), 1024, 512)
    return out.reshape(lead + (w.shape[1],))


def layer_norm(x, g, b):
    xf = x.astype(jnp.float32)
    mu = jnp.mean(xf, axis=-1, keepdims=True)
    var = jnp.mean(jnp.square(xf - mu), axis=-1, keepdims=True)
    return ((xf - mu) * lax.rsqrt(var + LN_EPS) * g + b).astype(x.dtype)


def masked_softmax(s, mask):
    s = jnp.where(mask, s.astype(jnp.float32), -jnp.inf)
    m = jnp.max(s, axis=-1, keepdims=True)
    e = jnp.exp(s - jnp.where(jnp.isfinite(m), m, 0.0))
    den = jnp.sum(e, axis=-1, keepdims=True)
    return e / jnp.where(den > 0.0, den, 1.0)


def rope(x, pos):
    half = ROT_DIM // 2
    inv = ROPE_THETA ** (-jnp.arange(half, dtype=jnp.float32) / half)
    ang = pos.astype(jnp.float32)[:, None] * inv
    cos = jnp.cos(ang)[:, None, :]
    sin = jnp.sin(ang)[:, None, :]
    xr = x[..., :ROT_DIM].astype(jnp.float32)
    x1, x2 = xr[..., :half], xr[..., half:]
    rot = jnp.concatenate([x1 * cos - x2 * sin, x2 * cos + x1 * sin], axis=-1).astype(x.dtype)
    return jnp.concatenate([rot, x[..., ROT_DIM:]], axis=-1)


def pool_mixer(u, u_prev, pos0, w_grp, scale):
    B, L, P = u.shape
    ext = jnp.concatenate([u_prev, u], axis=1)
    cs = jnp.pad(jnp.cumsum(ext.astype(jnp.float32), axis=1), ((0, 0), (1, 0), (0, 0)))
    pos = pos0 + np.arange(L)
    upto = cs[:, POOL_STATE + 1:]
    means = []
    for gi, w in enumerate(POOL_WINDOWS):
        c = slice(gi * POOL_GROUP, (gi + 1) * POOL_GROUP)
        before = cs[:, POOL_STATE + 1 - w:POOL_STATE + 1 - w + L, c]
        cnt = np.minimum(w, pos + 1).astype(np.float32)[None, :, None]
        means.append((upto[..., c] - before) / cnt)
    d = (jnp.concatenate(means, axis=-1) - u.astype(jnp.float32)).astype(u.dtype)
    d = d.reshape(B, L, N_POOL_GROUPS, POOL_GROUP)
    y = jnp.einsum('blgc,gce->blge', d, w_grp).reshape(B, L, P) * scale
    return y, ext[:, -POOL_STATE:]


def compress(kv, pe, w1, w2):
    B, Lt = kv.shape[:2]
    n_chunk = Lt // CMP_STRIDE
    n_cmp = n_chunk - CMP_RATIO + 1
    ch = kv[:, :n_chunk * CMP_STRIDE].reshape(B, n_chunk, CMP_STRIDE, N_KV, HEAD_DIM)
    w1r = w1.reshape(CMP_RATIO, CMP_STRIDE, HEAD_DIM, HEAD_DIM)
    a = jnp.einsum('bcsgd,rsde->rbcge', ch, w1r)
    hid = jnp.einsum('jd,jde->e', pe, w1)
    for r in range(CMP_RATIO):
        hid = hid + a[r, :, r:r + n_cmp]
    return jnp.einsum('bnge,ef->bngf', jax.nn.gelu(hid, approximate=False), w2)


def cmp_attend(q, qpos, kcb, vcb):
    B, L = q.shape[:2]
    n_cmp = kcb.shape[1]
    qg = q.reshape(B, L, N_KV, HPG, HEAD_DIM)
    s = jnp.einsum('blghd,bngd->bghln', qg, kcb) * ATTN_SCALE
    ends = np.arange(n_cmp) * CMP_STRIDE + CMP_LEN - 1
    pr = masked_softmax(s, ends[None, :] <= qpos[:, None])
    o = jnp.einsum('bghln,bngd->blghd', pr.astype(vcb.dtype), vcb).reshape(B, L, N_HEADS, HEAD_DIM)
    return o, jnp.sum(pr, axis=2)


def select_blocks(imp_c, qpos, Lt):
    n_cmp = imp_c.shape[-1]
    n_slc = -(-Lt // SLC_BLK)
    cs = np.arange(n_cmp) * CMP_STRIDE
    bs = np.arange(n_slc) * SLC_BLK
    inter = ((cs[:, None] < bs[None, :] + SLC_BLK) & (cs[:, None] + CMP_LEN > bs[None, :])).astype(np.float32)
    imp = jnp.einsum('bgln,nj->bglj', imp_c, jnp.asarray(inter))
    cur = (qpos // SLC_BLK)[:, None]
    j = np.arange(n_slc)[None, :]
    valid = bs[None, :] <= qpos[:, None]
    forced = valid & ((j == 0) | (j == cur) | (j == cur - 1))
    score = jnp.where(forced, jnp.inf, jnp.where(valid, imp, -jnp.inf))
    _, idx = lax.top_k(score, min(N_SEL, n_slc))
    return idx


def slc_attend(q, qpos, idx, k, v):
    B, L = q.shape[:2]
    Lt = k.shape[1]
    n_slc = -(-Lt // SLC_BLK)
    padk = ((0, 0), (0, n_slc * SLC_BLK - Lt), (0, 0), (0, 0))

    def blocks(a):
        return jnp.pad(a, padk).reshape(B, n_slc, SLC_BLK, N_KV, HEAD_DIM).transpose(0, 3, 1, 2, 4)

    kb, vb = blocks(k), blocks(v)
    n_q = B * L
    n_pad = (-n_q) % SLC_QCHUNK

    def flat(a):
        a = jnp.pad(a, [(0, n_pad)] + [(0, 0)] * (a.ndim - 1))
        return a.reshape((-1, SLC_QCHUNK) + a.shape[1:])

    qf = flat(q.reshape(n_q, N_KV, HPG, HEAD_DIM))
    idf = flat(idx.transpose(0, 2, 1, 3).reshape(n_q, N_KV, -1))
    sid = flat(jnp.repeat(jnp.arange(B, dtype=jnp.int32), L))
    pf = flat(jnp.asarray(np.broadcast_to(qpos, (B, L)).reshape(-1), jnp.int32))
    g_ix = jnp.arange(N_KV)[None, :, None]

    def step(args):
        qc, ic, sc, pc = args
        C, _, n_sel = ic.shape
        kg = kb[sc[:, None, None], g_ix, ic]
        vg = vb[sc[:, None, None], g_ix, ic]
        s = jnp.einsum('cghd,cgnkd->cghnk', qc, kg).reshape(C, N_KV, HPG, n_sel * SLC_BLK) * ATTN_SCALE
        kpos = ic[..., None] * SLC_BLK + jnp.arange(SLC_BLK)
        mask = (kpos <= pc[:, None, None, None]).reshape(C, N_KV, 1, n_sel * SLC_BLK)
        pr = masked_softmax(s, mask)
        vflat = vg.reshape(C, N_KV, n_sel * SLC_BLK, HEAD_DIM)
        return jnp.einsum('cghm,cgmd->cghd', pr.astype(vg.dtype), vflat)

    o = lax.map(step, (qf, idf, sid, pf))
    return o.reshape(-1, N_HEADS, HEAD_DIM)[:n_q].reshape(B, L, N_HEADS, HEAD_DIM)


def window_attend(q, k_lead, v_lead, k_new, v_new, pos0):
    B, L = q.shape[:2]
    kf = jnp.concatenate([k_lead, k_new], axis=1)
    vf = jnp.concatenate([v_lead, v_new], axis=1)
    kpos = jnp.asarray(pos0 - WINDOW + np.arange(WINDOW + L), jnp.int32)
    qb = WIN_QBLK if L % WIN_QBLK == 0 else L
    nqb = L // qb
    qblocks = q.reshape(B, nqb, qb, N_KV, HPG, HEAD_DIM).transpose(1, 0, 2, 3, 4, 5)

    def step(args):
        b, qc = args
        start = b * qb
        kc = lax.dynamic_slice_in_dim(kf, start, WINDOW + qb, axis=1)
        vc = lax.dynamic_slice_in_dim(vf, start, WINDOW + qb, axis=1)
        kp = lax.dynamic_slice_in_dim(kpos, start, WINDOW + qb, axis=0)[None, :]
        qp = (pos0 + start + jnp.arange(qb))[:, None]
        s = jnp.einsum('bighd,bkgd->bghik', qc, kc) * ATTN_SCALE
        mask = (kp <= qp) & (kp > qp - WINDOW) & (kp >= 0)
        pr = masked_softmax(s, mask)
        return jnp.einsum('bghik,bkgd->bighd', pr.astype(vc.dtype), vc)

    o = lax.map(step, (jnp.arange(nqb, dtype=jnp.int32), qblocks))
    return o.transpose(1, 0, 2, 3, 4, 5).reshape(B, L, N_HEADS, HEAD_DIM)


PEER_ROWS = PEER_HEADS * PEER_TOPK
PEER_TOK = 8
PEER_SLOTS = 4
PEER_AHEAD = PEER_SLOTS - 1
assert PEER_TOK % PEER_SLOTS == 0


def _peer_kernel(eid_ref, eid_next_ref, t_ref, gate_ref, uv_hbm, o_ref, buf, sem, *, n_steps):
    step = pl.program_id(0)
    last = n_steps - 1

    def start_token(ids_ref, j, slot):
        for k in range(PEER_ROWS):
            pltpu.make_async_copy(uv_hbm.at[pl.ds(ids_ref[j, k], 1)], buf.at[slot, pl.ds(k, 1)],
                                  sem.at[slot]).start()

    def wait_token(slot):
        pltpu.make_async_copy(uv_hbm.at[pl.ds(0, PEER_ROWS)], buf.at[slot], sem.at[slot]).wait()

    @pl.when(step == 0)
    def _():
        for j in range(PEER_AHEAD):
            start_token(eid_ref, j, j % PEER_SLOTS)

    for j in range(PEER_TOK):
        ahead = j + PEER_AHEAD
        if ahead < PEER_TOK:
            start_token(eid_ref, ahead, ahead % PEER_SLOTS)
        else:
            @pl.when(step < last)
            def _():
                start_token(eid_next_ref, ahead - PEER_TOK, ahead % PEER_SLOTS)
        slot = j % PEER_SLOTS
        wait_token(slot)
        act = jnp.sum(buf[slot, :, :D_MODEL] * t_ref[j:j + 1, :], axis=1, keepdims=True)
        w = gate_ref[0][:, j:j + 1] * (0.5 * act * (1.0 + lax.erf(act * (2.0 ** -0.5))))
        o_ref[j:j + 1, :] = jnp.sum(w * buf[slot, :, D_MODEL:], axis=0, keepdims=True)


def peer_experts(t, eid, gate, uv_tab):
    T, D = t.shape
    assert T % PEER_TOK == 0
    n_steps = T // PEER_TOK
    eid2 = eid.reshape(T, PEER_ROWS).astype(jnp.int32)
    gate_t = gate.reshape(n_steps, PEER_TOK, PEER_ROWS).transpose(0, 2, 1)
    return pl.pallas_call(
        functools.partial(_peer_kernel, n_steps=n_steps),
        grid=(n_steps,),
        in_specs=[pl.BlockSpec((PEER_TOK, PEER_ROWS), lambda i: (i, 0), memory_space=pltpu.SMEM),
                  pl.BlockSpec((PEER_TOK, PEER_ROWS), lambda i: (jnp.minimum(i + 1, n_steps - 1), 0),
                               memory_space=pltpu.SMEM),
                  pl.BlockSpec((PEER_TOK, D), lambda i: (i, 0)),
                  pl.BlockSpec((1, PEER_ROWS, PEER_TOK), lambda i: (i, 0, 0)),
                  pl.BlockSpec(memory_space=pl.ANY)],
        out_specs=pl.BlockSpec((PEER_TOK, D), lambda i: (i, 0)),
        out_shape=jax.ShapeDtypeStruct((T, D), jnp.float32),
        scratch_shapes=[pltpu.VMEM((PEER_SLOTS, PEER_ROWS, 2 * D), jnp.float32),
                        pltpu.SemaphoreType.DMA((PEER_SLOTS,))],
        compiler_params=pltpu.CompilerParams(
            dimension_semantics=("arbitrary",),
            vmem_limit_bytes=PEER_SLOTS * PEER_ROWS * 2 * D * 4 + (16 << 20)),
        name="peer_experts",
    )(eid2, eid2, t, gate_t, uv_tab)


def peer(h, wq, sub_keys, uv_tab):
    B, L, D = h.shape
    t = h.reshape(B * L, D)
    T = t.shape[0]
    q = dense(t, wq).reshape(T, PEER_HEADS, 2, PEER_HALF)
    s = jnp.einsum('thpc,hpkc->thpk', q, sub_keys).astype(jnp.float32)
    s1, i1 = lax.top_k(s[:, :, 0], PEER_TOPK)
    s2, i2 = lax.top_k(s[:, :, 1], PEER_TOPK)
    cand = (s1[..., :, None] + s2[..., None, :]).reshape(T, PEER_HEADS, PEER_TOPK * PEER_TOPK)
    cid = (i1[..., :, None] * N_KEYS + i2[..., None, :]).reshape(T, PEER_HEADS, PEER_TOPK * PEER_TOPK)
    top, sel = lax.top_k(cand, PEER_TOPK)
    eid = jnp.take_along_axis(cid, sel, axis=-1)
    gate = jax.nn.softmax(top, axis=-1)
    return peer_experts(t, eid, gate, uv_tab).reshape(B, L, D)


def decoder_layer(x, pos0, past, win_k, win_v, pool_prev, win_rows, p):
    B, L, _ = x.shape
    dt = x.dtype
    qpos = pos0 + np.arange(L)
    qpos_j = jnp.asarray(qpos, jnp.int32)
    proj = input_projection(x, p['w_in'])
    splits = np.cumsum(IN_SIZES)[:-1].tolist()
    u_pool, q, kc, vc, ks, vs, kw, vw, g_nsa, g_br = jnp.split(proj, splits, axis=-1)
    q = q.reshape(B, L, N_HEADS, HEAD_DIM)
    kc, vc, ks, vs, kw, vw = [a.reshape(B, L, N_KV, HEAD_DIM) for a in (kc, vc, ks, vs, kw, vw)]
    q_rot = rope(q, qpos_j)
    ks = rope(ks, qpos_j)
    kw = rope(kw, qpos_j)
    if past is None:
        kc_all, vc_all, ks_all, vs_all = kc, vc, ks, vs
    else:
        kc_all, vc_all, ks_all, vs_all = [jnp.concatenate([a, b], axis=1) for a, b in zip(past, (kc, vc, ks, vs))]
    kcb = compress(kc_all, p['cmp_pe_k'], p['cmp_w1_k'], p['cmp_w2_k'])
    vcb = compress(vc_all, p['cmp_pe_v'], p['cmp_w1_v'], p['cmp_w2_v'])
    o_cmp, imp = cmp_attend(q, qpos, kcb, vcb)
    idx = select_blocks(imp, qpos, ks_all.shape[1])
    if past is None:
        o_slc = prompt_attention(q_rot, ks, vs, selection_bias(idx, -(-L // SLC_BLK)), None)
        o_win = prompt_attention(q_rot, kw, vw, None, WINDOW)
    else:
        o_slc = slc_attend(q_rot, qpos, idx, ks_all, vs_all)
        lead = ((0, 0), (WINDOW - win_k.shape[1], 0), (0, 0), (0, 0))
        o_win = window_attend(q_rot, jnp.pad(win_k, lead), jnp.pad(win_v, lead), kw, vw, pos0)
    g = jax.nn.sigmoid(g_nsa.astype(jnp.float32)).astype(dt).reshape(B, L, 3, N_HEADS, 1)
    o_attn = (g[:, :, 0] * o_cmp + g[:, :, 1] * o_slc + g[:, :, 2] * o_win).reshape(B, L, Q_WIDTH)
    o_pool, pool_new = pool_mixer(u_pool, pool_prev, pos0, p['pool_grp_w'], p['pool_scale'])
    gate = jax.nn.sigmoid(g_br.astype(jnp.float32)).astype(dt)
    merged = (gate[..., :D_MODEL] * dense(o_pool, p['w_branch_pool'])
              + gate[..., D_MODEL:] * dense(o_attn, p['w_branch_attn']))
    h = layer_norm(DN_ALPHA * x + dense(merged, p['w_out']), p['ln1_g'], p['ln1_b'])
    z = peer(h, p['peer_wq'], p['peer_subkeys'], p['peer_uv'])
    y = layer_norm(DN_ALPHA * h + z, p['ln2_g'], p['ln2_b'])
    win_k_new = jnp.concatenate([win_k, kw], axis=1)[:, -win_rows:]
    win_v_new = jnp.concatenate([win_v, vw], axis=1)[:, -win_rows:]
    return y, (kc, vc, ks, vs, win_k_new, win_v_new, pool_new)


def kernel(x_prompt, x_sample, cache_k_cmp, cache_v_cmp, cache_k_slc, cache_v_slc, cache_k_win, cache_v_win, state_pool, page_table, w_in, pool_grp_w, pool_scale, cmp_pe_k, cmp_w1_k, cmp_w2_k, cmp_pe_v, cmp_w1_v, cmp_w2_v, w_branch_pool, w_branch_attn, w_out, ln1_g, ln1_b, peer_wq, peer_subkeys, peer_u, peer_v, ln2_g, ln2_b):
    B, L = x_prompt.shape[:2]
    zero_win = jnp.zeros((B, 0, N_KV, HEAD_DIM), x_prompt.dtype)
    zero_pool = jnp.zeros((B, POOL_STATE, POOL_WIDTH), x_prompt.dtype)

    def gather_pages(c):
        g = c[page_table]
        return g.reshape(g.shape[0], -1, N_KV, HEAD_DIM)

    hp, hs = x_prompt, x_sample
    st_prompt, st_sample = [], []
    for l in range(DEPTH):
        p = {'w_in': w_in[l], 'pool_grp_w': pool_grp_w[l], 'pool_scale': pool_scale[l],
             'cmp_pe_k': cmp_pe_k[l], 'cmp_w1_k': cmp_w1_k[l], 'cmp_w2_k': cmp_w2_k[l],
             'cmp_pe_v': cmp_pe_v[l], 'cmp_w1_v': cmp_w1_v[l], 'cmp_w2_v': cmp_w2_v[l],
             'w_branch_pool': w_branch_pool[l], 'w_branch_attn': w_branch_attn[l], 'w_out': w_out[l],
             'ln1_g': ln1_g[l], 'ln1_b': ln1_b[l], 'peer_wq': peer_wq[l], 'peer_subkeys': peer_subkeys[l],
             'peer_uv': jnp.concatenate([peer_u[l], peer_v[l]], axis=1),
             'ln2_g': ln2_g[l], 'ln2_b': ln2_b[l]}
        hp, sp = decoder_layer(hp, 0, None, zero_win, zero_win, zero_pool, min(WINDOW, L), p)
        past = (gather_pages(cache_k_cmp[l]), gather_pages(cache_v_cmp[l]),
                gather_pages(cache_k_slc[l]), gather_pages(cache_v_slc[l]))
        hs, ss = decoder_layer(hs, PAST_LEN, past, cache_k_win[l], cache_v_win[l], state_pool[l],
                               cache_k_win.shape[2], p)
        st_prompt.append(sp)
        st_sample.append(ss)

    def stack(states, i):
        return jnp.stack([s[i] for s in states], axis=0)

    return (hp, hs) + tuple(stack(st_prompt, i) for i in range(7)) + tuple(stack(st_sample, i) for i in range(7))
```

```python
import functools

import numpy as np
import jax
import jax.numpy as jnp
from jax import lax
from jax.experimental import pallas as pl
from jax.experimental.pallas import tpu as pltpu


D_MODEL = 2048
BATCH = 8
SEQ = 2048
DEPTH = 1
DEC_BATCH = 128
DEC_SEQ = 4
PAST_LEN = 2048
PAGE_SIZE = 128

POOL_WINDOWS = (2, 4, 8, 16)
N_POOL_GROUPS = len(POOL_WINDOWS)
POOL_WIDTH = D_MODEL // 2
POOL_GROUP = POOL_WIDTH // N_POOL_GROUPS
POOL_STATE = max(POOL_WINDOWS) - 1
N_HEADS = 16
HEAD_DIM = 128
N_KV = 4
HPG = N_HEADS // N_KV
ROT_DIM = HEAD_DIM // 4
ROPE_THETA = 500000.0
ATTN_SCALE = HEAD_DIM ** -0.5
CMP_LEN = 32
CMP_STRIDE = 16
CMP_RATIO = CMP_LEN // CMP_STRIDE
SLC_BLK = 64
N_SEL = 16
WINDOW = 512
WIN_QBLK = 128
SLC_QCHUNK = 64
PEER_HEADS = 8
N_KEYS = 128
N_EXPERTS = N_KEYS * N_KEYS
PEER_TOPK = 16
PEER_QDIM = 256
PEER_HALF = PEER_QDIM // 2
PEER_BLOCK = 64
DN_ALPHA = (2.0 * DEPTH) ** 0.25
DN_BETA = (8.0 * DEPTH) ** -0.25
LN_EPS = 1e-5

Q_WIDTH = N_HEADS * HEAD_DIM
KV_WIDTH = N_KV * HEAD_DIM
IN_SIZES = (POOL_WIDTH, Q_WIDTH, KV_WIDTH, KV_WIDTH, KV_WIDTH, KV_WIDTH, KV_WIDTH, KV_WIDTH, 3 * N_HEADS, 2 * D_MODEL)

V7X_LANES = 128


def _matmul_kernel(a_ref, b_ref, o_ref):
    o_ref[...] = jnp.dot(a_ref[...], b_ref[...], preferred_element_type=jnp.float32)


def matmul_bf16(a, b, tm, tn):
    M, K = a.shape
    _, N = b.shape
    tm, tn = min(tm, M), min(tn, N)
    assert M % tm == 0 and N % tn == 0
    vmem = 2 * (tm * K * 2 + K * tn * 2 + tm * tn * 4) + (4 << 20)
    return pl.pallas_call(
        _matmul_kernel,
        grid=(M // tm, N // tn),
        in_specs=[pl.BlockSpec((tm, K), lambda i, j: (i, 0)),
                  pl.BlockSpec((K, tn), lambda i, j: (0, j))],
        out_specs=pl.BlockSpec((tm, tn), lambda i, j: (i, j)),
        out_shape=jax.ShapeDtypeStruct((M, N), jnp.float32),
        compiler_params=pltpu.CompilerParams(
            dimension_semantics=("parallel", "arbitrary"), vmem_limit_bytes=vmem),
        name="matmul_bf16",
    )(a, b)


def input_projection(x, w_in):
    B, L, D = x.shape
    xb = x.reshape(B * L, D).astype(jnp.bfloat16)
    g0 = sum(IN_SIZES[:8])
    g1 = g0 + IN_SIZES[8]
    w_main = jnp.concatenate([w_in[:, :g0], w_in[:, g1:]], axis=1).astype(jnp.bfloat16)
    w_gate = jnp.pad(w_in[:, g0:g1], ((0, 0), (0, V7X_LANES - IN_SIZES[8]))).astype(jnp.bfloat16)
    main = matmul_bf16(xb, w_main, 1024, 512).reshape(B, L, -1)
    gate = matmul_bf16(xb, w_gate, 1024, V7X_LANES).reshape(B, L, -1)
    splits = np.cumsum(IN_SIZES[:8]).tolist()
    return jnp.split(main, splits, axis=-1)[:8] + [gate[..., :IN_SIZES[8]], main[..., g0:]]


ATTN_TQ = 128
ATTN_TK = 256
MASK_BIAS = -30000.0
NEG_BIG = -1e30


def _attn_kernel(*refs, tq, tk, window, n_blk):
    if n_blk:
        q_ref, k_ref, v_ref, sb_ref, o_ref, m_sc, l_sc, acc_sc = refs
    else:
        q_ref, k_ref, v_ref, o_ref, m_sc, l_sc, acc_sc = refs
    q0 = pl.program_id(2) * tq
    rows = HPG * tq
    q = (q_ref[0] * ATTN_SCALE).astype(jnp.bfloat16).reshape(rows, HEAD_DIM)
    m_sc[...] = jnp.full(m_sc.shape, NEG_BIG, jnp.float32)
    l_sc[...] = jnp.zeros(l_sc.shape, jnp.float32)
    acc_sc[...] = jnp.zeros(acc_sc.shape, jnp.float32)
    qpos = q0 + lax.broadcasted_iota(jnp.int32, (tq, tk), 0)
    hi = (q0 + tq - 1) // tk
    lo = 0 if window is None else jnp.maximum(q0 - (window - 1), 0) // tk

    def body(kt, carry):
        k0 = pl.multiple_of(kt * tk, tk)
        kb = k_ref[0, pl.ds(k0, tk), :].astype(jnp.bfloat16)
        vb = v_ref[0, pl.ds(k0, tk), :].astype(jnp.bfloat16)
        s = lax.dot_general(q, kb, (((1,), (1,)), ((), ())), preferred_element_type=jnp.float32)
        kpos = k0 + lax.broadcasted_iota(jnp.int32, (tq, tk), 1)
        ok = kpos <= qpos
        if window is not None:
            ok = ok & (kpos > qpos - window)
        if n_blk:
            kblk = (k0 + lax.broadcasted_iota(jnp.int32, (n_blk, tk), 1)) // SLC_BLK
            expand = (kblk == lax.broadcasted_iota(jnp.int32, (n_blk, tk), 0)).astype(jnp.bfloat16)
            bias = jnp.where(ok, jnp.dot(sb_ref[0, 0], expand, preferred_element_type=jnp.float32), NEG_BIG)
        else:
            bias = jnp.where(ok, 0.0, NEG_BIG)
        s = (s.reshape(HPG, tq, tk) + bias[None]).reshape(rows, tk)
        m_old = m_sc[...]
        m_new = jnp.maximum(m_old, jnp.max(s, axis=-1, keepdims=True))
        alpha = jnp.exp(m_old - m_new)
        p = jnp.exp(s - m_new)
        l_sc[...] = alpha * l_sc[...] + jnp.sum(p, axis=-1, keepdims=True)
        acc_sc[...] = alpha * acc_sc[...] + jnp.dot(p.astype(jnp.bfloat16), vb, preferred_element_type=jnp.float32)
        m_sc[...] = m_new
        return carry

    lax.fori_loop(lo, hi + 1, body, 0)
    o_ref[0] = (acc_sc[...] / l_sc[...]).reshape(HPG, tq, HEAD_DIM)


def prompt_attention(q, k, v, sel_bias, window):
    B, L = q.shape[:2]
    tq, tk = min(ATTN_TQ, L), min(ATTN_TK, L)
    assert L % tq == 0 and L % tk == 0
    qh = q.transpose(0, 2, 1, 3)
    kf = k.reshape(B, L, KV_WIDTH)
    vf = v.reshape(B, L, KV_WIDTH)
    n_blk = 0 if sel_bias is None else sel_bias.shape[-1]
    in_specs = [pl.BlockSpec((1, HPG, tq, HEAD_DIM), lambda b, g, i: (b, g, i, 0)),
                pl.BlockSpec((1, L, HEAD_DIM), lambda b, g, i: (b, 0, g)),
                pl.BlockSpec((1, L, HEAD_DIM), lambda b, g, i: (b, 0, g))]
    args = [qh, kf, vf]
    if n_blk:
        in_specs.append(pl.BlockSpec((1, 1, tq, n_blk), lambda b, g, i: (b, g, i, 0)))
        args.append(sel_bias)
    rows = HPG * tq
    o = pl.pallas_call(
        functools.partial(_attn_kernel, tq=tq, tk=tk, window=window, n_blk=n_blk),
        grid=(B, N_KV, L // tq),
        in_specs=in_specs,
        out_specs=pl.BlockSpec((1, HPG, tq, HEAD_DIM), lambda b, g, i: (b, g, i, 0)),
        out_shape=jax.ShapeDtypeStruct((B, N_HEADS, L, HEAD_DIM), jnp.float32),
        scratch_shapes=[pltpu.VMEM((rows, 1), jnp.float32), pltpu.VMEM((rows, 1), jnp.float32),
                        pltpu.VMEM((rows, HEAD_DIM), jnp.float32)],
        compiler_params=pltpu.CompilerParams(
            dimension_semantics=("parallel", "parallel", "arbitrary"), vmem_limit_bytes=32 << 20),
        name="slc_attention" if n_blk else "win_attention",
    )(*args)
    return o.transpose(0, 2, 1, 3)


def selection_bias(idx, n_slc):
    sel = jnp.any(idx[..., None] == jnp.arange(n_slc, dtype=idx.dtype), axis=-2)
    return jnp.where(sel, 0.0, MASK_BIAS).astype(jnp.bfloat16)


def dense(a, w):
    lead = a.shape[:-1]
    out = matmul_bf16(a.reshape(-1, a.shape[-1]).astype(jnp.bfloat16), w.astype(jnp.bfloat16), 1024, 512)
    return out.reshape(lead + (w.shape[1],))


def layer_norm(x, g, b):
    xf = x.astype(jnp.float32)
    mu = jnp.mean(xf, axis=-1, keepdims=True)
    var = jnp.mean(jnp.square(xf - mu), axis=-1, keepdims=True)
    return ((xf - mu) * lax.rsqrt(var + LN_EPS) * g + b).astype(x.dtype)


def masked_softmax(s, mask):
    s = jnp.where(mask, s.astype(jnp.float32), -jnp.inf)
    m = jnp.max(s, axis=-1, keepdims=True)
    e = jnp.exp(s - jnp.where(jnp.isfinite(m), m, 0.0))
    den = jnp.sum(e, axis=-1, keepdims=True)
    return e / jnp.where(den > 0.0, den, 1.0)


def rope(x, pos):
    half = ROT_DIM // 2
    inv = ROPE_THETA ** (-jnp.arange(half, dtype=jnp.float32) / half)
    ang = pos.astype(jnp.float32)[:, None] * inv
    cos = jnp.cos(ang)[:, None, :]
    sin = jnp.sin(ang)[:, None, :]
    xr = x[..., :ROT_DIM].astype(jnp.float32)
    x1, x2 = xr[..., :half], xr[..., half:]
    rot = jnp.concatenate([x1 * cos - x2 * sin, x2 * cos + x1 * sin], axis=-1).astype(x.dtype)
    return jnp.concatenate([rot, x[..., ROT_DIM:]], axis=-1)


def pool_mixer(u, u_prev, pos0, w_grp, scale):
    B, L, P = u.shape
    ext = jnp.concatenate([u_prev, u], axis=1)
    cs = jnp.pad(jnp.cumsum(ext.astype(jnp.float32), axis=1), ((0, 0), (1, 0), (0, 0)))
    pos = pos0 + np.arange(L)
    upto = cs[:, POOL_STATE + 1:]
    means = []
    for gi, w in enumerate(POOL_WINDOWS):
        c = slice(gi * POOL_GROUP, (gi + 1) * POOL_GROUP)
        before = cs[:, POOL_STATE + 1 - w:POOL_STATE + 1 - w + L, c]
        cnt = np.minimum(w, pos + 1).astype(np.float32)[None, :, None]
        means.append((upto[..., c] - before) / cnt)
    d = (jnp.concatenate(means, axis=-1) - u.astype(jnp.float32)).astype(u.dtype)
    d = d.reshape(B, L, N_POOL_GROUPS, POOL_GROUP)
    y = jnp.einsum('blgc,gce->blge', d, w_grp).reshape(B, L, P) * scale
    return y, ext[:, -POOL_STATE:]


def compress(kv, pe, w1, w2):
    B, Lt = kv.shape[:2]
    n_chunk = Lt // CMP_STRIDE
    n_cmp = n_chunk - CMP_RATIO + 1
    ch = kv[:, :n_chunk * CMP_STRIDE].reshape(B, n_chunk, CMP_STRIDE, N_KV, HEAD_DIM)
    w1r = w1.reshape(CMP_RATIO, CMP_STRIDE, HEAD_DIM, HEAD_DIM)
    a = jnp.einsum('bcsgd,rsde->rbcge', ch, w1r)
    hid = jnp.einsum('jd,jde->e', pe, w1)
    for r in range(CMP_RATIO):
        hid = hid + a[r, :, r:r + n_cmp]
    return jnp.einsum('bnge,ef->bngf', jax.nn.gelu(hid, approximate=False), w2)


def cmp_attend(q, qpos, kcb, vcb):
    B, L = q.shape[:2]
    n_cmp = kcb.shape[1]
    qg = q.reshape(B, L, N_KV, HPG, HEAD_DIM)
    s = jnp.einsum('blghd,bngd->bghln', qg, kcb) * ATTN_SCALE
    ends = np.arange(n_cmp) * CMP_STRIDE + CMP_LEN - 1
    pr = masked_softmax(s, ends[None, :] <= qpos[:, None])
    o = jnp.einsum('bghln,bngd->blghd', pr.astype(vcb.dtype), vcb).reshape(B, L, N_HEADS, HEAD_DIM)
    return o, jnp.sum(pr, axis=2)


def select_blocks(imp_c, qpos, Lt):
    n_cmp = imp_c.shape[-1]
    n_slc = -(-Lt // SLC_BLK)
    cs = np.arange(n_cmp) * CMP_STRIDE
    bs = np.arange(n_slc) * SLC_BLK
    inter = ((cs[:, None] < bs[None, :] + SLC_BLK) & (cs[:, None] + CMP_LEN > bs[None, :])).astype(np.float32)
    imp = jnp.einsum('bgln,nj->bglj', imp_c, jnp.asarray(inter))
    cur = (qpos // SLC_BLK)[:, None]
    j = np.arange(n_slc)[None, :]
    valid = bs[None, :] <= qpos[:, None]
    forced = valid & ((j == 0) | (j == cur) | (j == cur - 1))
    score = jnp.where(forced, jnp.inf, jnp.where(valid, imp, -jnp.inf))
    _, idx = lax.top_k(score, min(N_SEL, n_slc))
    return idx


def slc_attend(q, qpos, idx, k, v):
    B, L = q.shape[:2]
    Lt = k.shape[1]
    n_slc = -(-Lt // SLC_BLK)
    padk = ((0, 0), (0, n_slc * SLC_BLK - Lt), (0, 0), (0, 0))

    def blocks(a):
        return jnp.pad(a, padk).reshape(B, n_slc, SLC_BLK, N_KV, HEAD_DIM).transpose(0, 3, 1, 2, 4)

    kb, vb = blocks(k), blocks(v)
    n_q = B * L
    n_pad = (-n_q) % SLC_QCHUNK

    def flat(a):
        a = jnp.pad(a, [(0, n_pad)] + [(0, 0)] * (a.ndim - 1))
        return a.reshape((-1, SLC_QCHUNK) + a.shape[1:])

    qf = flat(q.reshape(n_q, N_KV, HPG, HEAD_DIM))
    idf = flat(idx.transpose(0, 2, 1, 3).reshape(n_q, N_KV, -1))
    sid = flat(jnp.repeat(jnp.arange(B, dtype=jnp.int32), L))
    pf = flat(jnp.asarray(np.broadcast_to(qpos, (B, L)).reshape(-1), jnp.int32))
    g_ix = jnp.arange(N_KV)[None, :, None]

    def step(args):
        qc, ic, sc, pc = args
        C, _, n_sel = ic.shape
        kg = kb[sc[:, None, None], g_ix, ic]
        vg = vb[sc[:, None, None], g_ix, ic]
        s = jnp.einsum('cghd,cgnkd->cghnk', qc, kg).reshape(C, N_KV, HPG, n_sel * SLC_BLK) * ATTN_SCALE
        kpos = ic[..., None] * SLC_BLK + jnp.arange(SLC_BLK)
        mask = (kpos <= pc[:, None, None, None]).reshape(C, N_KV, 1, n_sel * SLC_BLK)
        pr = masked_softmax(s, mask)
        vflat = vg.reshape(C, N_KV, n_sel * SLC_BLK, HEAD_DIM)
        return jnp.einsum('cghm,cgmd->cghd', pr.astype(vg.dtype), vflat)

    o = lax.map(step, (qf, idf, sid, pf))
    return o.reshape(-1, N_HEADS, HEAD_DIM)[:n_q].reshape(B, L, N_HEADS, HEAD_DIM)


def window_attend(q, k_lead, v_lead, k_new, v_new, pos0):
    B, L = q.shape[:2]
    kf = jnp.concatenate([k_lead, k_new], axis=1)
    vf = jnp.concatenate([v_lead, v_new], axis=1)
    kpos = jnp.asarray(pos0 - WINDOW + np.arange(WINDOW + L), jnp.int32)
    qb = WIN_QBLK if L % WIN_QBLK == 0 else L
    nqb = L // qb
    qblocks = q.reshape(B, nqb, qb, N_KV, HPG, HEAD_DIM).transpose(1, 0, 2, 3, 4, 5)

    def step(args):
        b, qc = args
        start = b * qb
        kc = lax.dynamic_slice_in_dim(kf, start, WINDOW + qb, axis=1)
        vc = lax.dynamic_slice_in_dim(vf, start, WINDOW + qb, axis=1)
        kp = lax.dynamic_slice_in_dim(kpos, start, WINDOW + qb, axis=0)[None, :]
        qp = (pos0 + start + jnp.arange(qb))[:, None]
        s = jnp.einsum('bighd,bkgd->bghik', qc, kc) * ATTN_SCALE
        mask = (kp <= qp) & (kp > qp - WINDOW) & (kp >= 0)
        pr = masked_softmax(s, mask)
        return jnp.einsum('bghik,bkgd->bighd', pr.astype(vc.dtype), vc)

    o = lax.map(step, (jnp.arange(nqb, dtype=jnp.int32), qblocks))
    return o.transpose(1, 0, 2, 3, 4, 5).reshape(B, L, N_HEADS, HEAD_DIM)


PEER_ROWS = PEER_HEADS * PEER_TOPK
PEER_TOK = 8
PEER_SLOTS = 4
PEER_AHEAD = PEER_SLOTS - 1
assert PEER_TOK % PEER_SLOTS == 0


def _peer_kernel(eid_ref, eid_next_ref, t_ref, gate_ref, uv_hbm, o_ref, buf, sem, *, n_steps):
    step = pl.program_id(0)
    last = n_steps - 1

    def start_token(ids_ref, j, slot):
        for k in range(PEER_ROWS):
            pltpu.make_async_copy(uv_hbm.at[pl.ds(ids_ref[j, k], 1)], buf.at[slot, pl.ds(k, 1)],
                                  sem.at[slot]).start()

    def wait_token(slot):
        pltpu.make_async_copy(uv_hbm.at[pl.ds(0, PEER_ROWS)], buf.at[slot], sem.at[slot]).wait()

    @pl.when(step == 0)
    def _():
        for j in range(PEER_AHEAD):
            start_token(eid_ref, j, j % PEER_SLOTS)

    for j in range(PEER_TOK):
        ahead = j + PEER_AHEAD
        if ahead < PEER_TOK:
            start_token(eid_ref, ahead, ahead % PEER_SLOTS)
        else:
            @pl.when(step < last)
            def _():
                start_token(eid_next_ref, ahead - PEER_TOK, ahead % PEER_SLOTS)
        slot = j % PEER_SLOTS
        wait_token(slot)
        act = jnp.sum(buf[slot, :, :D_MODEL] * t_ref[j:j + 1, :], axis=1, keepdims=True)
        w = gate_ref[0][:, j:j + 1] * (0.5 * act * (1.0 + lax.erf(act * (2.0 ** -0.5))))
        o_ref[j:j + 1, :] = jnp.sum(w * buf[slot, :, D_MODEL:], axis=0, keepdims=True)


ROUTE_TT = 256


def _top_rows(s, k, write):
    n = s.shape[0]
    rows = lax.broadcasted_iota(jnp.int32, s.shape, 0).astype(jnp.float32)
    for r in range(k):
        m = jnp.max(s, axis=0, keepdims=True)
        i = jnp.min(jnp.where(s == m, rows, float(n)), axis=0, keepdims=True)
        hit = rows == i
        write(r, m, i, hit)
        s = jnp.where(hit, -jnp.inf, s)


def _route_kernel(q_ref, keys_ref, eid_ref, gate_ref, v1_sc, i1_sc, v2_sc, i2_sc, cand_sc, cid_sc, top_sc):
    for p, (v_sc, i_sc) in enumerate(((v1_sc, i1_sc), (v2_sc, i2_sc))):
        s = lax.dot_general(keys_ref[0, p], q_ref[:, p * PEER_HALF:(p + 1) * PEER_HALF],
                            (((1,), (1,)), ((), ())), preferred_element_type=jnp.float32)

        def write_half(r, m, i, hit, v_sc=v_sc, i_sc=i_sc):
            v_sc[r:r + 1, :] = m
            i_sc[r:r + 1, :] = i

        _top_rows(s, PEER_TOPK, write_half)
    for a in range(PEER_TOPK):
        cand_sc[a * PEER_TOPK:(a + 1) * PEER_TOPK, :] = v1_sc[a:a + 1, :] + v2_sc[...]
        cid_sc[a * PEER_TOPK:(a + 1) * PEER_TOPK, :] = i1_sc[a:a + 1, :] * float(N_KEYS) + i2_sc[...]
    cid = cid_sc[...]

    def write_top(r, m, i, hit):
        top_sc[r:r + 1, :] = m
        eid_ref[0, r:r + 1, :] = jnp.sum(jnp.where(hit, cid, 0.0), axis=0, keepdims=True).astype(jnp.int32)

    _top_rows(cand_sc[...], PEER_TOPK, write_top)
    top = top_sc[...]
    e = jnp.exp(top - jnp.max(top, axis=0, keepdims=True))
    gate_ref[0] = e / jnp.sum(e, axis=0, keepdims=True)


def peer_route(q, sub_keys):
    T = q.shape[0]
    tt = min(ROUTE_TT, T)
    assert T % tt == 0
    f32 = jnp.float32
    return pl.pallas_call(
        _route_kernel,
        grid=(T // tt, PEER_HEADS),
        in_specs=[pl.BlockSpec((tt, PEER_QDIM), lambda i, h: (i, h)),
                  pl.BlockSpec((1, 2, N_KEYS, PEER_HALF), lambda i, h: (h, 0, 0, 0))],
        out_specs=[pl.BlockSpec((1, PEER_TOPK, tt), lambda i, h: (h, 0, i)),
                   pl.BlockSpec((1, PEER_TOPK, tt), lambda i, h: (h, 0, i))],
        out_shape=[jax.ShapeDtypeStruct((PEER_HEADS, PEER_TOPK, T), jnp.int32),
                   jax.ShapeDtypeStruct((PEER_HEADS, PEER_TOPK, T), f32)],
        scratch_shapes=[pltpu.VMEM((PEER_TOPK, tt), f32), pltpu.VMEM((PEER_TOPK, tt), f32),
                        pltpu.VMEM((PEER_TOPK, tt), f32), pltpu.VMEM((PEER_TOPK, tt), f32),
                        pltpu.VMEM((PEER_TOPK * PEER_TOPK, tt), f32), pltpu.VMEM((PEER_TOPK * PEER_TOPK, tt), f32),
                        pltpu.VMEM((PEER_TOPK, tt), f32)],
        compiler_params=pltpu.CompilerParams(dimension_semantics=("parallel", "parallel")),
        name="peer_route",
    )(q.astype(jnp.bfloat16), sub_keys.astype(jnp.bfloat16))


def peer_experts(t, eid, gate_rows, uv_tab):
    T, D = t.shape
    assert T % PEER_TOK == 0
    n_steps = T // PEER_TOK
    eid2 = eid
    gate_t = gate_rows.reshape(PEER_ROWS, n_steps, PEER_TOK).transpose(1, 0, 2)
    return pl.pallas_call(
        functools.partial(_peer_kernel, n_steps=n_steps),
        grid=(n_steps,),
        in_specs=[pl.BlockSpec((PEER_TOK, PEER_ROWS), lambda i: (i, 0), memory_space=pltpu.SMEM),
                  pl.BlockSpec((PEER_TOK, PEER_ROWS), lambda i: (jnp.minimum(i + 1, n_steps - 1), 0),
                               memory_space=pltpu.SMEM),
                  pl.BlockSpec((PEER_TOK, D), lambda i: (i, 0)),
                  pl.BlockSpec((1, PEER_ROWS, PEER_TOK), lambda i: (i, 0, 0)),
                  pl.BlockSpec(memory_space=pl.ANY)],
        out_specs=pl.BlockSpec((PEER_TOK, D), lambda i: (i, 0)),
        out_shape=jax.ShapeDtypeStruct((T, D), jnp.float32),
        scratch_shapes=[pltpu.VMEM((PEER_SLOTS, PEER_ROWS, 2 * D), jnp.float32),
                        pltpu.SemaphoreType.DMA((PEER_SLOTS,))],
        compiler_params=pltpu.CompilerParams(
            dimension_semantics=("arbitrary",),
            vmem_limit_bytes=PEER_SLOTS * PEER_ROWS * 2 * D * 4 + (16 << 20)),
        name="peer_experts",
    )(eid2, eid2, t, gate_t, uv_tab)


def peer(h, wq, sub_keys, uv_tab):
    B, L, D = h.shape
    t = h.reshape(B * L, D)
    T = t.shape[0]
    eid, gate = peer_route(dense(t, wq), sub_keys)
    return peer_experts(t, eid.reshape(PEER_ROWS, T).T, gate.reshape(PEER_ROWS, T), uv_tab).reshape(B, L, D)


def decoder_layer(x, pos0, past, win_k, win_v, pool_prev, win_rows, p):
    B, L, _ = x.shape
    dt = x.dtype
    qpos = pos0 + np.arange(L)
    qpos_j = jnp.asarray(qpos, jnp.int32)
    u_pool, q, kc, vc, ks, vs, kw, vw, g_nsa, g_br = input_projection(x, p['w_in'])
    q = q.reshape(B, L, N_HEADS, HEAD_DIM)
    kc, vc, ks, vs, kw, vw = [a.reshape(B, L, N_KV, HEAD_DIM) for a in (kc, vc, ks, vs, kw, vw)]
    q_rot = rope(q, qpos_j)
    ks = rope(ks, qpos_j)
    kw = rope(kw, qpos_j)
    if past is None:
        kc_all, vc_all, ks_all, vs_all = kc, vc, ks, vs
    else:
        kc_all, vc_all, ks_all, vs_all = [jnp.concatenate([a, b], axis=1) for a, b in zip(past, (kc, vc, ks, vs))]
    kcb = compress(kc_all, p['cmp_pe_k'], p['cmp_w1_k'], p['cmp_w2_k'])
    vcb = compress(vc_all, p['cmp_pe_v'], p['cmp_w1_v'], p['cmp_w2_v'])
    o_cmp, imp = cmp_attend(q, qpos, kcb, vcb)
    idx = select_blocks(imp, qpos, ks_all.shape[1])
    if past is None:
        o_slc = prompt_attention(q_rot, ks, vs, selection_bias(idx, -(-L // SLC_BLK)), None)
        o_win = prompt_attention(q_rot, kw, vw, None, WINDOW)
    else:
        o_slc = slc_attend(q_rot, qpos, idx, ks_all, vs_all)
        lead = ((0, 0), (WINDOW - win_k.shape[1], 0), (0, 0), (0, 0))
        o_win = window_attend(q_rot, jnp.pad(win_k, lead), jnp.pad(win_v, lead), kw, vw, pos0)
    g = jax.nn.sigmoid(g_nsa.astype(jnp.float32)).astype(dt).reshape(B, L, 3, N_HEADS, 1)
    o_attn = (g[:, :, 0] * o_cmp + g[:, :, 1] * o_slc + g[:, :, 2] * o_win).reshape(B, L, Q_WIDTH)
    o_pool, pool_new = pool_mixer(u_pool, pool_prev, pos0, p['pool_grp_w'], p['pool_scale'])
    gate = jax.nn.sigmoid(g_br.astype(jnp.float32)).astype(dt)
    merged = (gate[..., :D_MODEL] * dense(o_pool, p['w_branch_pool'])
              + gate[..., D_MODEL:] * dense(o_attn, p['w_branch_attn']))
    h = layer_norm(DN_ALPHA * x + dense(merged, p['w_out']), p['ln1_g'], p['ln1_b'])
    z = peer(h, p['peer_wq'], p['peer_subkeys'], p['peer_uv'])
    y = layer_norm(DN_ALPHA * h + z, p['ln2_g'], p['ln2_b'])
    win_k_new = jnp.concatenate([win_k, kw], axis=1)[:, -win_rows:]
    win_v_new = jnp.concatenate([win_v, vw], axis=1)[:, -win_rows:]
    return y, (kc, vc, ks, vs, win_k_new, win_v_new, pool_new)


def kernel(x_prompt, x_sample, cache_k_cmp, cache_v_cmp, cache_k_slc, cache_v_slc, cache_k_win, cache_v_win, state_pool, page_table, w_in, pool_grp_w, pool_scale, cmp_pe_k, cmp_w1_k, cmp_w2_k, cmp_pe_v, cmp_w1_v, cmp_w2_v, w_branch_pool, w_branch_attn, w_out, ln1_g, ln1_b, peer_wq, peer_subkeys, peer_u, peer_v, ln2_g, ln2_b):
    B, L = x_prompt.shape[:2]
    zero_win = jnp.zeros((B, 0, N_KV, HEAD_DIM), x_prompt.dtype)
    zero_pool = jnp.zeros((B, POOL_STATE, POOL_WIDTH), x_prompt.dtype)

    def gather_pages(c):
        g = c[page_table]
        return g.reshape(g.shape[0], -1, N_KV, HEAD_DIM)

    hp, hs = x_prompt, x_sample
    st_prompt, st_sample = [], []
    for l in range(DEPTH):
        p = {'w_in': w_in[l], 'pool_grp_w': pool_grp_w[l], 'pool_scale': pool_scale[l],
             'cmp_pe_k': cmp_pe_k[l], 'cmp_w1_k': cmp_w1_k[l], 'cmp_w2_k': cmp_w2_k[l],
             'cmp_pe_v': cmp_pe_v[l], 'cmp_w1_v': cmp_w1_v[l], 'cmp_w2_v': cmp_w2_v[l],
             'w_branch_pool': w_branch_pool[l], 'w_branch_attn': w_branch_attn[l], 'w_out': w_out[l],
             'ln1_g': ln1_g[l], 'ln1_b': ln1_b[l], 'peer_wq': peer_wq[l], 'peer_subkeys': peer_subkeys[l],
             'peer_uv': jnp.concatenate([peer_u[l], peer_v[l]], axis=1),
             'ln2_g': ln2_g[l], 'ln2_b': ln2_b[l]}
        hp, sp = decoder_layer(hp, 0, None, zero_win, zero_win, zero_pool, min(WINDOW, L), p)
        past = (gather_pages(cache_k_cmp[l]), gather_pages(cache_v_cmp[l]),
                gather_pages(cache_k_slc[l]), gather_pages(cache_v_slc[l]))
        hs, ss = decoder_layer(hs, PAST_LEN, past, cache_k_win[l], cache_v_win[l], state_pool[l],
                               cache_k_win.shape[2], p)
        st_prompt.append(sp)
        st_sample.append(ss)

    def stack(states, i):
        return jnp.stack([s[i] for s in states], axis=0)

    return (hp, hs) + tuple(stack(st_prompt, i) for i in range(7)) + tuple(stack(st_sample, i) for i in range(7))
```

```python
import functools

import numpy as np
import jax
import jax.numpy as jnp
from jax import lax
from jax.experimental import pallas as pl
from jax.experimental.pallas import tpu as pltpu


D_MODEL = 2048
BATCH = 8
SEQ = 2048
DEPTH = 1
DEC_BATCH = 128
DEC_SEQ = 4
PAST_LEN = 2048
PAGE_SIZE = 128

POOL_WINDOWS = (2, 4, 8, 16)
N_POOL_GROUPS = len(POOL_WINDOWS)
POOL_WIDTH = D_MODEL // 2
POOL_GROUP = POOL_WIDTH // N_POOL_GROUPS
POOL_STATE = max(POOL_WINDOWS) - 1
N_HEADS = 16
HEAD_DIM = 128
N_KV = 4
HPG = N_HEADS // N_KV
ROT_DIM = HEAD_DIM // 4
ROPE_THETA = 500000.0
ATTN_SCALE = HEAD_DIM ** -0.5
CMP_LEN = 32
CMP_STRIDE = 16
CMP_RATIO = CMP_LEN // CMP_STRIDE
SLC_BLK = 64
N_SEL = 16
WINDOW = 512
WIN_QBLK = 128
SLC_QCHUNK = 64
PEER_HEADS = 8
N_KEYS = 128
N_EXPERTS = N_KEYS * N_KEYS
PEER_TOPK = 16
PEER_QDIM = 256
PEER_HALF = PEER_QDIM // 2
PEER_BLOCK = 64
DN_ALPHA = (2.0 * DEPTH) ** 0.25
DN_BETA = (8.0 * DEPTH) ** -0.25
LN_EPS = 1e-5

Q_WIDTH = N_HEADS * HEAD_DIM
KV_WIDTH = N_KV * HEAD_DIM
IN_SIZES = (POOL_WIDTH, Q_WIDTH, KV_WIDTH, KV_WIDTH, KV_WIDTH, KV_WIDTH, KV_WIDTH, KV_WIDTH, 3 * N_HEADS, 2 * D_MODEL)

V7X_LANES = 128


def _matmul_kernel(a_ref, b_ref, o_ref):
    o_ref[...] = jnp.dot(a_ref[...], b_ref[...], preferred_element_type=jnp.float32)


def matmul_bf16(a, b, tm, tn):
    M, K = a.shape
    _, N = b.shape
    tm, tn = min(tm, M), min(tn, N)
    assert M % tm == 0 and N % tn == 0
    vmem = 2 * (tm * K * 2 + K * tn * 2 + tm * tn * 4) + (4 << 20)
    return pl.pallas_call(
        _matmul_kernel,
        grid=(M // tm, N // tn),
        in_specs=[pl.BlockSpec((tm, K), lambda i, j: (i, 0)),
                  pl.BlockSpec((K, tn), lambda i, j: (0, j))],
        out_specs=pl.BlockSpec((tm, tn), lambda i, j: (i, j)),
        out_shape=jax.ShapeDtypeStruct((M, N), jnp.float32),
        compiler_params=pltpu.CompilerParams(
            dimension_semantics=("parallel", "arbitrary"), vmem_limit_bytes=vmem),
        name="matmul_bf16",
    )(a, b)


def input_projection(x, w_in):
    B, L, D = x.shape
    xb = x.reshape(B * L, D).astype(jnp.bfloat16)
    g0 = sum(IN_SIZES[:8])
    g1 = g0 + IN_SIZES[8]
    w_main = jnp.concatenate([w_in[:, :g0], w_in[:, g1:]], axis=1).astype(jnp.bfloat16)
    w_gate = jnp.pad(w_in[:, g0:g1], ((0, 0), (0, V7X_LANES - IN_SIZES[8]))).astype(jnp.bfloat16)
    main = matmul_bf16(xb, w_main, 1024, 512).reshape(B, L, -1)
    gate = matmul_bf16(xb, w_gate, 1024, V7X_LANES).reshape(B, L, -1)
    splits = np.cumsum(IN_SIZES[:8]).tolist()
    return jnp.split(main, splits, axis=-1)[:8] + [gate[..., :IN_SIZES[8]], main[..., g0:]]


ATTN_TQ = 128
ATTN_TK = 512
MASK_BIAS = -30000.0
NEG_BIG = -1e30


def _attn_kernel(*refs, tq, tk, window, n_blk):
    if n_blk:
        q_ref, k_ref, v_ref, sb_ref, o_ref, m_sc, l_sc, acc_sc = refs
    else:
        q_ref, k_ref, v_ref, o_ref, m_sc, l_sc, acc_sc = refs
    q0 = pl.program_id(2) * tq
    rows = HPG * tq
    q = (q_ref[0] * ATTN_SCALE).astype(jnp.bfloat16).reshape(rows, HEAD_DIM)
    m_sc[...] = jnp.full(m_sc.shape, NEG_BIG, jnp.float32)
    l_sc[...] = jnp.zeros(l_sc.shape, jnp.float32)
    acc_sc[...] = jnp.zeros(acc_sc.shape, jnp.float32)
    qpos = q0 + lax.broadcasted_iota(jnp.int32, (tq, tk), 0)
    hi = (q0 + tq - 1) // tk
    lo = 0 if window is None else jnp.maximum(q0 - (window - 1), 0) // tk

    def body(kt, carry):
        k0 = pl.multiple_of(kt * tk, tk)
        kb = k_ref[0, pl.ds(k0, tk), :].astype(jnp.bfloat16)
        vb = v_ref[0, pl.ds(k0, tk), :].astype(jnp.bfloat16)
        s = lax.dot_general(q, kb, (((1,), (1,)), ((), ())), preferred_element_type=jnp.float32)
        kpos = k0 + lax.broadcasted_iota(jnp.int32, (tq, tk), 1)
        ok = kpos <= qpos
        if window is not None:
            ok = ok & (kpos > qpos - window)
        if n_blk:
            kblk = (k0 + lax.broadcasted_iota(jnp.int32, (n_blk, tk), 1)) // SLC_BLK
            expand = (kblk == lax.broadcasted_iota(jnp.int32, (n_blk, tk), 0)).astype(jnp.bfloat16)
            bias = jnp.where(ok, jnp.dot(sb_ref[0, 0], expand, preferred_element_type=jnp.float32), NEG_BIG)
        else:
            bias = jnp.where(ok, 0.0, NEG_BIG)
        s = (s.reshape(HPG, tq, tk) + bias[None]).reshape(rows, tk)
        m_old = m_sc[...]
        m_new = jnp.maximum(m_old, jnp.max(s, axis=-1, keepdims=True))
        alpha = jnp.exp(m_old - m_new)
        p = jnp.exp(s - m_new)
        l_sc[...] = alpha * l_sc[...] + jnp.sum(p, axis=-1, keepdims=True)
        acc_sc[...] = alpha * acc_sc[...] + jnp.dot(p.astype(jnp.bfloat16), vb, preferred_element_type=jnp.float32)
        m_sc[...] = m_new
        return carry

    lax.fori_loop(lo, hi + 1, body, 0)
    o_ref[0] = (acc_sc[...] / l_sc[...]).reshape(HPG, tq, HEAD_DIM)


def prompt_attention(q, k, v, sel_bias, window):
    B, L = q.shape[:2]
    tq, tk = min(ATTN_TQ, L), min(ATTN_TK, L)
    assert L % tq == 0 and L % tk == 0
    qh = q.transpose(0, 2, 1, 3)
    kf = k.reshape(B, L, KV_WIDTH)
    vf = v.reshape(B, L, KV_WIDTH)
    n_blk = 0 if sel_bias is None else sel_bias.shape[-1]
    in_specs = [pl.BlockSpec((1, HPG, tq, HEAD_DIM), lambda b, g, i: (b, g, i, 0)),
                pl.BlockSpec((1, L, HEAD_DIM), lambda b, g, i: (b, 0, g)),
                pl.BlockSpec((1, L, HEAD_DIM), lambda b, g, i: (b, 0, g))]
    args = [qh, kf, vf]
    if n_blk:
        in_specs.append(pl.BlockSpec((1, 1, tq, n_blk), lambda b, g, i: (b, g, i, 0)))
        args.append(sel_bias)
    rows = HPG * tq
    o = pl.pallas_call(
        functools.partial(_attn_kernel, tq=tq, tk=tk, window=window, n_blk=n_blk),
        grid=(B, N_KV, L // tq),
        in_specs=in_specs,
        out_specs=pl.BlockSpec((1, HPG, tq, HEAD_DIM), lambda b, g, i: (b, g, i, 0)),
        out_shape=jax.ShapeDtypeStruct((B, N_HEADS, L, HEAD_DIM), jnp.float32),
        scratch_shapes=[pltpu.VMEM((rows, 1), jnp.float32), pltpu.VMEM((rows, 1), jnp.float32),
                        pltpu.VMEM((rows, HEAD_DIM), jnp.float32)],
        compiler_params=pltpu.CompilerParams(
            dimension_semantics=("parallel", "parallel", "arbitrary"), vmem_limit_bytes=32 << 20),
        name="slc_attention" if n_blk else "win_attention",
    )(*args)
    return o.transpose(0, 2, 1, 3)


def selection_bias(idx, n_slc):
    sel = jnp.any(idx[..., None] == jnp.arange(n_slc, dtype=idx.dtype), axis=-2)
    return jnp.where(sel, 0.0, MASK_BIAS).astype(jnp.bfloat16)


def dense(a, w):
    lead = a.shape[:-1]
    out = matmul_bf16(a.reshape(-1, a.shape[-1]).astype(jnp.bfloat16), w.astype(jnp.bfloat16), 1024, 512)
    return out.reshape(lead + (w.shape[1],))


def layer_norm(x, g, b):
    xf = x.astype(jnp.float32)
    mu = jnp.mean(xf, axis=-1, keepdims=True)
    var = jnp.mean(jnp.square(xf - mu), axis=-1, keepdims=True)
    return ((xf - mu) * lax.rsqrt(var + LN_EPS) * g + b).astype(x.dtype)


def masked_softmax(s, mask):
    s = jnp.where(mask, s.astype(jnp.float32), -jnp.inf)
    m = jnp.max(s, axis=-1, keepdims=True)
    e = jnp.exp(s - jnp.where(jnp.isfinite(m), m, 0.0))
    den = jnp.sum(e, axis=-1, keepdims=True)
    return e / jnp.where(den > 0.0, den, 1.0)


def rope(x, pos):
    half = ROT_DIM // 2
    inv = ROPE_THETA ** (-jnp.arange(half, dtype=jnp.float32) / half)
    ang = pos.astype(jnp.float32)[:, None] * inv
    cos = jnp.cos(ang)[:, None, :]
    sin = jnp.sin(ang)[:, None, :]
    xr = x[..., :ROT_DIM].astype(jnp.float32)
    x1, x2 = xr[..., :half], xr[..., half:]
    rot = jnp.concatenate([x1 * cos - x2 * sin, x2 * cos + x1 * sin], axis=-1).astype(x.dtype)
    return jnp.concatenate([rot, x[..., ROT_DIM:]], axis=-1)


def pool_mixer(u, u_prev, pos0, w_grp, scale):
    B, L, P = u.shape
    ext = jnp.concatenate([u_prev, u], axis=1)
    cs = jnp.pad(jnp.cumsum(ext.astype(jnp.float32), axis=1), ((0, 0), (1, 0), (0, 0)))
    pos = pos0 + np.arange(L)
    upto = cs[:, POOL_STATE + 1:]
    means = []
    for gi, w in enumerate(POOL_WINDOWS):
        c = slice(gi * POOL_GROUP, (gi + 1) * POOL_GROUP)
        before = cs[:, POOL_STATE + 1 - w:POOL_STATE + 1 - w + L, c]
        cnt = np.minimum(w, pos + 1).astype(np.float32)[None, :, None]
        means.append((upto[..., c] - before) / cnt)
    d = (jnp.concatenate(means, axis=-1) - u.astype(jnp.float32)).astype(u.dtype)
    d = d.reshape(B, L, N_POOL_GROUPS, POOL_GROUP)
    y = jnp.einsum('blgc,gce->blge', d, w_grp).reshape(B, L, P) * scale
    return y, ext[:, -POOL_STATE:]


def compress(kv, pe, w1, w2):
    B, Lt = kv.shape[:2]
    n_chunk = Lt // CMP_STRIDE
    n_cmp = n_chunk - CMP_RATIO + 1
    ch = kv[:, :n_chunk * CMP_STRIDE].reshape(B, n_chunk, CMP_STRIDE, N_KV, HEAD_DIM)
    w1r = w1.reshape(CMP_RATIO, CMP_STRIDE, HEAD_DIM, HEAD_DIM)
    a = jnp.einsum('bcsgd,rsde->rbcge', ch, w1r)
    hid = jnp.einsum('jd,jde->e', pe, w1)
    for r in range(CMP_RATIO):
        hid = hid + a[r, :, r:r + n_cmp]
    return jnp.einsum('bnge,ef->bngf', jax.nn.gelu(hid, approximate=False), w2)


def cmp_attend(q, qpos, kcb, vcb):
    B, L = q.shape[:2]
    n_cmp = kcb.shape[1]
    qg = q.reshape(B, L, N_KV, HPG, HEAD_DIM)
    s = jnp.einsum('blghd,bngd->bghln', qg, kcb) * ATTN_SCALE
    ends = np.arange(n_cmp) * CMP_STRIDE + CMP_LEN - 1
    pr = masked_softmax(s, ends[None, :] <= qpos[:, None])
    o = jnp.einsum('bghln,bngd->blghd', pr.astype(vcb.dtype), vcb).reshape(B, L, N_HEADS, HEAD_DIM)
    return o, jnp.sum(pr, axis=2)


def select_blocks(imp_c, qpos, Lt):
    n_cmp = imp_c.shape[-1]
    n_slc = -(-Lt // SLC_BLK)
    cs = np.arange(n_cmp) * CMP_STRIDE
    bs = np.arange(n_slc) * SLC_BLK
    inter = ((cs[:, None] < bs[None, :] + SLC_BLK) & (cs[:, None] + CMP_LEN > bs[None, :])).astype(np.float32)
    imp = jnp.einsum('bgln,nj->bglj', imp_c, jnp.asarray(inter))
    cur = (qpos // SLC_BLK)[:, None]
    j = np.arange(n_slc)[None, :]
    valid = bs[None, :] <= qpos[:, None]
    forced = valid & ((j == 0) | (j == cur) | (j == cur - 1))
    score = jnp.where(forced, jnp.inf, jnp.where(valid, imp, -jnp.inf))
    _, idx = lax.top_k(score, min(N_SEL, n_slc))
    return idx


def slc_attend(q, qpos, idx, k, v):
    B, L = q.shape[:2]
    Lt = k.shape[1]
    n_slc = -(-Lt // SLC_BLK)
    padk = ((0, 0), (0, n_slc * SLC_BLK - Lt), (0, 0), (0, 0))

    def blocks(a):
        return jnp.pad(a, padk).reshape(B, n_slc, SLC_BLK, N_KV, HEAD_DIM).transpose(0, 3, 1, 2, 4)

    kb, vb = blocks(k), blocks(v)
    n_q = B * L
    n_pad = (-n_q) % SLC_QCHUNK

    def flat(a):
        a = jnp.pad(a, [(0, n_pad)] + [(0, 0)] * (a.ndim - 1))
        return a.reshape((-1, SLC_QCHUNK) + a.shape[1:])

    qf = flat(q.reshape(n_q, N_KV, HPG, HEAD_DIM))
    idf = flat(idx.transpose(0, 2, 1, 3).reshape(n_q, N_KV, -1))
    sid = flat(jnp.repeat(jnp.arange(B, dtype=jnp.int32), L))
    pf = flat(jnp.asarray(np.broadcast_to(qpos, (B, L)).reshape(-1), jnp.int32))
    g_ix = jnp.arange(N_KV)[None, :, None]

    def step(args):
        qc, ic, sc, pc = args
        C, _, n_sel = ic.shape
        kg = kb[sc[:, None, None], g_ix, ic]
        vg = vb[sc[:, None, None], g_ix, ic]
        s = jnp.einsum('cghd,cgnkd->cghnk', qc, kg).reshape(C, N_KV, HPG, n_sel * SLC_BLK) * ATTN_SCALE
        kpos = ic[..., None] * SLC_BLK + jnp.arange(SLC_BLK)
        mask = (kpos <= pc[:, None, None, None]).reshape(C, N_KV, 1, n_sel * SLC_BLK)
        pr = masked_softmax(s, mask)
        vflat = vg.reshape(C, N_KV, n_sel * SLC_BLK, HEAD_DIM)
        return jnp.einsum('cghm,cgmd->cghd', pr.astype(vg.dtype), vflat)

    o = lax.map(step, (qf, idf, sid, pf))
    return o.reshape(-1, N_HEADS, HEAD_DIM)[:n_q].reshape(B, L, N_HEADS, HEAD_DIM)


def window_attend(q, k_lead, v_lead, k_new, v_new, pos0):
    B, L = q.shape[:2]
    kf = jnp.concatenate([k_lead, k_new], axis=1)
    vf = jnp.concatenate([v_lead, v_new], axis=1)
    kpos = jnp.asarray(pos0 - WINDOW + np.arange(WINDOW + L), jnp.int32)
    qb = WIN_QBLK if L % WIN_QBLK == 0 else L
    nqb = L // qb
    qblocks = q.reshape(B, nqb, qb, N_KV, HPG, HEAD_DIM).transpose(1, 0, 2, 3, 4, 5)

    def step(args):
        b, qc = args
        start = b * qb
        kc = lax.dynamic_slice_in_dim(kf, start, WINDOW + qb, axis=1)
        vc = lax.dynamic_slice_in_dim(vf, start, WINDOW + qb, axis=1)
        kp = lax.dynamic_slice_in_dim(kpos, start, WINDOW + qb, axis=0)[None, :]
        qp = (pos0 + start + jnp.arange(qb))[:, None]
        s = jnp.einsum('bighd,bkgd->bghik', qc, kc) * ATTN_SCALE
        mask = (kp <= qp) & (kp > qp - WINDOW) & (kp >= 0)
        pr = masked_softmax(s, mask)
        return jnp.einsum('bghik,bkgd->bighd', pr.astype(vc.dtype), vc)

    o = lax.map(step, (jnp.arange(nqb, dtype=jnp.int32), qblocks))
    return o.transpose(1, 0, 2, 3, 4, 5).reshape(B, L, N_HEADS, HEAD_DIM)


PEER_ROWS = PEER_HEADS * PEER_TOPK
PEER_TOK = 8
PEER_SLOTS = 4
PEER_AHEAD = PEER_SLOTS - 1
assert PEER_TOK % PEER_SLOTS == 0


def _peer_kernel(eid_ref, eid_next_ref, t_ref, gate_ref, uv_hbm, o_ref, buf, sem, *, n_steps):
    step = pl.program_id(0)
    last = n_steps - 1

    def start_token(ids_ref, j, slot):
        for k in range(PEER_ROWS):
            pltpu.make_async_copy(uv_hbm.at[pl.ds(ids_ref[j, k], 1)], buf.at[slot, pl.ds(k, 1)],
                                  sem.at[slot]).start()

    def wait_token(slot):
        pltpu.make_async_copy(uv_hbm.at[pl.ds(0, PEER_ROWS)], buf.at[slot], sem.at[slot]).wait()

    @pl.when(step == 0)
    def _():
        for j in range(PEER_AHEAD):
            start_token(eid_ref, j, j % PEER_SLOTS)

    for j in range(PEER_TOK):
        ahead = j + PEER_AHEAD
        if ahead < PEER_TOK:
            start_token(eid_ref, ahead, ahead % PEER_SLOTS)
        else:
            @pl.when(step < last)
            def _():
                start_token(eid_next_ref, ahead - PEER_TOK, ahead % PEER_SLOTS)
        slot = j % PEER_SLOTS
        wait_token(slot)
        act = jnp.sum(buf[slot, :, :D_MODEL] * t_ref[j:j + 1, :], axis=1, keepdims=True)
        w = gate_ref[0][:, j:j + 1] * (0.5 * act * (1.0 + lax.erf(act * (2.0 ** -0.5))))
        o_ref[j:j + 1, :] = jnp.sum(w * buf[slot, :, D_MODEL:], axis=0, keepdims=True)


ROUTE_TT = 256


def _top_rows(s, k, write):
    n = s.shape[0]
    rows = lax.broadcasted_iota(jnp.int32, s.shape, 0).astype(jnp.float32)
    for r in range(k):
        m = jnp.max(s, axis=0, keepdims=True)
        i = jnp.min(jnp.where(s == m, rows, float(n)), axis=0, keepdims=True)
        hit = rows == i
        write(r, m, i, hit)
        s = jnp.where(hit, -jnp.inf, s)


def _route_kernel(q_ref, keys_ref, eid_ref, gate_ref, v1_sc, i1_sc, v2_sc, i2_sc, cand_sc, cid_sc, top_sc):
    for p, (v_sc, i_sc) in enumerate(((v1_sc, i1_sc), (v2_sc, i2_sc))):
        s = lax.dot_general(keys_ref[0, p], q_ref[:, p * PEER_HALF:(p + 1) * PEER_HALF],
                            (((1,), (1,)), ((), ())), preferred_element_type=jnp.float32)

        def write_half(r, m, i, hit, v_sc=v_sc, i_sc=i_sc):
            v_sc[r:r + 1, :] = m
            i_sc[r:r + 1, :] = i

        _top_rows(s, PEER_TOPK, write_half)
    for a in range(PEER_TOPK):
        cand_sc[a * PEER_TOPK:(a + 1) * PEER_TOPK, :] = v1_sc[a:a + 1, :] + v2_sc[...]
        cid_sc[a * PEER_TOPK:(a + 1) * PEER_TOPK, :] = i1_sc[a:a + 1, :] * float(N_KEYS) + i2_sc[...]
    cid = cid_sc[...]

    def write_top(r, m, i, hit):
        top_sc[r:r + 1, :] = m
        eid_ref[0, r:r + 1, :] = jnp.sum(jnp.where(hit, cid, 0.0), axis=0, keepdims=True).astype(jnp.int32)

    _top_rows(cand_sc[...], PEER_TOPK, write_top)
    top = top_sc[...]
    e = jnp.exp(top - jnp.max(top, axis=0, keepdims=True))
    gate_ref[0] = e / jnp.sum(e, axis=0, keepdims=True)


def peer_route(q, sub_keys):
    T = q.shape[0]
    tt = min(ROUTE_TT, T)
    assert T % tt == 0
    f32 = jnp.float32
    return pl.pallas_call(
        _route_kernel,
        grid=(T // tt, PEER_HEADS),
        in_specs=[pl.BlockSpec((tt, PEER_QDIM), lambda i, h: (i, h)),
                  pl.BlockSpec((1, 2, N_KEYS, PEER_HALF), lambda i, h: (h, 0, 0, 0))],
        out_specs=[pl.BlockSpec((1, PEER_TOPK, tt), lambda i, h: (h, 0, i)),
                   pl.BlockSpec((1, PEER_TOPK, tt), lambda i, h: (h, 0, i))],
        out_shape=[jax.ShapeDtypeStruct((PEER_HEADS, PEER_TOPK, T), jnp.int32),
                   jax.ShapeDtypeStruct((PEER_HEADS, PEER_TOPK, T), f32)],
        scratch_shapes=[pltpu.VMEM((PEER_TOPK, tt), f32), pltpu.VMEM((PEER_TOPK, tt), f32),
                        pltpu.VMEM((PEER_TOPK, tt), f32), pltpu.VMEM((PEER_TOPK, tt), f32),
                        pltpu.VMEM((PEER_TOPK * PEER_TOPK, tt), f32), pltpu.VMEM((PEER_TOPK * PEER_TOPK, tt), f32),
                        pltpu.VMEM((PEER_TOPK, tt), f32)],
        compiler_params=pltpu.CompilerParams(dimension_semantics=("parallel", "parallel")),
        name="peer_route",
    )(q.astype(jnp.bfloat16), sub_keys.astype(jnp.bfloat16))


def peer_experts(t, eid, gate_rows, uv_tab):
    T, D = t.shape
    assert T % PEER_TOK == 0
    n_steps = T // PEER_TOK
    eid2 = eid
    gate_t = gate_rows.reshape(PEER_ROWS, n_steps, PEER_TOK).transpose(1, 0, 2)
    return pl.pallas_call(
        functools.partial(_peer_kernel, n_steps=n_steps),
        grid=(n_steps,),
        in_specs=[pl.BlockSpec((PEER_TOK, PEER_ROWS), lambda i: (i, 0), memory_space=pltpu.SMEM),
                  pl.BlockSpec((PEER_TOK, PEER_ROWS), lambda i: (jnp.minimum(i + 1, n_steps - 1), 0),
                               memory_space=pltpu.SMEM),
                  pl.BlockSpec((PEER_TOK, D), lambda i: (i, 0)),
                  pl.BlockSpec((1, PEER_ROWS, PEER_TOK), lambda i: (i, 0, 0)),
                  pl.BlockSpec(memory_space=pl.ANY)],
        out_specs=pl.BlockSpec((PEER_TOK, D), lambda i: (i, 0)),
        out_shape=jax.ShapeDtypeStruct((T, D), jnp.float32),
        scratch_shapes=[pltpu.VMEM((PEER_SLOTS, PEER_ROWS, 2 * D), jnp.float32),
                        pltpu.SemaphoreType.DMA((PEER_SLOTS,))],
        compiler_params=pltpu.CompilerParams(
            dimension_semantics=("arbitrary",),
            vmem_limit_bytes=PEER_SLOTS * PEER_ROWS * 2 * D * 4 + (16 << 20)),
        name="peer_experts",
    )(eid2, eid2, t, gate_t, uv_tab)


def peer(h, wq, sub_keys, uv_tab):
    B, L, D = h.shape
    t = h.reshape(B * L, D)
    T = t.shape[0]
    eid, gate = peer_route(dense(t, wq), sub_keys)
    return peer_experts(t, eid.reshape(PEER_ROWS, T).T, gate.reshape(PEER_ROWS, T), uv_tab).reshape(B, L, D)


def decoder_layer(x, pos0, past, win_k, win_v, pool_prev, win_rows, p):
    B, L, _ = x.shape
    dt = x.dtype
    qpos = pos0 + np.arange(L)
    qpos_j = jnp.asarray(qpos, jnp.int32)
    u_pool, q, kc, vc, ks, vs, kw, vw, g_nsa, g_br = input_projection(x, p['w_in'])
    q = q.reshape(B, L, N_HEADS, HEAD_DIM)
    kc, vc, ks, vs, kw, vw = [a.reshape(B, L, N_KV, HEAD_DIM) for a in (kc, vc, ks, vs, kw, vw)]
    q_rot = rope(q, qpos_j)
    ks = rope(ks, qpos_j)
    kw = rope(kw, qpos_j)
    if past is None:
        kc_all, vc_all, ks_all, vs_all = kc, vc, ks, vs
    else:
        kc_all, vc_all, ks_all, vs_all = [jnp.concatenate([a, b], axis=1) for a, b in zip(past, (kc, vc, ks, vs))]
    kcb = compress(kc_all, p['cmp_pe_k'], p['cmp_w1_k'], p['cmp_w2_k'])
    vcb = compress(vc_all, p['cmp_pe_v'], p['cmp_w1_v'], p['cmp_w2_v'])
    o_cmp, imp = cmp_attend(q, qpos, kcb, vcb)
    idx = select_blocks(imp, qpos, ks_all.shape[1])
    if past is None:
        o_slc = prompt_attention(q_rot, ks, vs, selection_bias(idx, -(-L // SLC_BLK)), None)
        o_win = prompt_attention(q_rot, kw, vw, None, WINDOW)
    else:
        o_slc = slc_attend(q_rot, qpos, idx, ks_all, vs_all)
        lead = ((0, 0), (WINDOW - win_k.shape[1], 0), (0, 0), (0, 0))
        o_win = window_attend(q_rot, jnp.pad(win_k, lead), jnp.pad(win_v, lead), kw, vw, pos0)
    g = jax.nn.sigmoid(g_nsa.astype(jnp.float32)).astype(dt).reshape(B, L, 3, N_HEADS, 1)
    o_attn = (g[:, :, 0] * o_cmp + g[:, :, 1] * o_slc + g[:, :, 2] * o_win).reshape(B, L, Q_WIDTH)
    o_pool, pool_new = pool_mixer(u_pool, pool_prev, pos0, p['pool_grp_w'], p['pool_scale'])
    gate = jax.nn.sigmoid(g_br.astype(jnp.float32)).astype(dt)
    merged = (gate[..., :D_MODEL] * dense(o_pool, p['w_branch_pool'])
              + gate[..., D_MODEL:] * dense(o_attn, p['w_branch_attn']))
    h = layer_norm(DN_ALPHA * x + dense(merged, p['w_out']), p['ln1_g'], p['ln1_b'])
    z = peer(h, p['peer_wq'], p['peer_subkeys'], p['peer_uv'])
    y = layer_norm(DN_ALPHA * h + z, p['ln2_g'], p['ln2_b'])
    win_k_new = jnp.concatenate([win_k, kw], axis=1)[:, -win_rows:]
    win_v_new = jnp.concatenate([win_v, vw], axis=1)[:, -win_rows:]
    return y, (kc, vc, ks, vs, win_k_new, win_v_new, pool_new)


def kernel(x_prompt, x_sample, cache_k_cmp, cache_v_cmp, cache_k_slc, cache_v_slc, cache_k_win, cache_v_win, state_pool, page_table, w_in, pool_grp_w, pool_scale, cmp_pe_k, cmp_w1_k, cmp_w2_k, cmp_pe_v, cmp_w1_v, cmp_w2_v, w_branch_pool, w_branch_attn, w_out, ln1_g, ln1_b, peer_wq, peer_subkeys, peer_u, peer_v, ln2_g, ln2_b):
    B, L = x_prompt.shape[:2]
    zero_win = jnp.zeros((B, 0, N_KV, HEAD_DIM), x_prompt.dtype)
    zero_pool = jnp.zeros((B, POOL_STATE, POOL_WIDTH), x_prompt.dtype)

    def gather_pages(c):
        g = c[page_table]
        return g.reshape(g.shape[0], -1, N_KV, HEAD_DIM)

    hp, hs = x_prompt, x_sample
    st_prompt, st_sample = [], []
    for l in range(DEPTH):
        p = {'w_in': w_in[l], 'pool_grp_w': pool_grp_w[l], 'pool_scale': pool_scale[l],
             'cmp_pe_k': cmp_pe_k[l], 'cmp_w1_k': cmp_w1_k[l], 'cmp_w2_k': cmp_w2_k[l],
             'cmp_pe_v': cmp_pe_v[l], 'cmp_w1_v': cmp_w1_v[l], 'cmp_w2_v': cmp_w2_v[l],
             'w_branch_pool': w_branch_pool[l], 'w_branch_attn': w_branch_attn[l], 'w_out': w_out[l],
             'ln1_g': ln1_g[l], 'ln1_b': ln1_b[l], 'peer_wq': peer_wq[l], 'peer_subkeys': peer_subkeys[l],
             'peer_uv': jnp.concatenate([peer_u[l], peer_v[l]], axis=1),
             'ln2_g': ln2_g[l], 'ln2_b': ln2_b[l]}
        hp, sp = decoder_layer(hp, 0, None, zero_win, zero_win, zero_pool, min(WINDOW, L), p)
        past = (gather_pages(cache_k_cmp[l]), gather_pages(cache_v_cmp[l]),
                gather_pages(cache_k_slc[l]), gather_pages(cache_v_slc[l]))
        hs, ss = decoder_layer(hs, PAST_LEN, past, cache_k_win[l], cache_v_win[l], state_pool[l],
                               cache_k_win.shape[2], p)
        st_prompt.append(sp)
        st_sample.append(ss)

    def stack(states, i):
        return jnp.stack([s[i] for s in states], axis=0)

    return (hp, hs) + tuple(stack(st_prompt, i) for i in range(7)) + tuple(stack(st_sample, i) for i in range(7))
```
